```python
import jax, jax.numpy as jnp
from jax import lax
import numpy as np


D_MODEL = 1024
BATCH = 8
SEQ = 4096
DEPTH = 2

CTX_LEN = 256
GRID_W = 64
A_HEADS = 4
A_HEAD_DIM = 128
A_WIDTH = A_HEADS * A_HEAD_DIM
A_CHUNK = 64
B_GROUPS = 4
B_GROUP_DIM = 128
B_WIDTH = B_GROUPS * B_GROUP_DIM
C_GROUPS = 4
C_GROUP_DIM = 128
C_WIDTH = C_GROUPS * C_GROUP_DIM
SGU_CHUNK = 128
ROWS_PER_CHUNK = SGU_CHUNK // GRID_W
N_BRANCH = 3
A_IN = 4 * A_WIDTH
IN_COLS = A_IN + A_WIDTH + B_WIDTH + 2 * C_WIDTH + N_BRANCH * D_MODEL
N_EXPERTS = 16
D_EXPERT = 1024
CAPACITY_FACTOR = 2
EPS = 1e-6
LB_FLOOR = 1e-20

kernel_name = 'hybrid_hgrn2_fourier_sgu_ecmoe_diffusion_block'


def rmsnorm(x, w):
    xf = x.astype(jnp.float32)
    y = xf * lax.rsqrt(jnp.mean(xf * xf, axis=-1, keepdims=True) + EPS)
    return (y * w.astype(jnp.float32)).astype(x.dtype)


def adaln(cvec, w, b):
    return jnp.split(jax.nn.silu(cvec) @ w + b, 6, axis=-1)


def modulate(h, shift, scale):
    return h * (1.0 + scale) + shift


def log_forget(z, lb):
    zf = z.astype(jnp.float32)
    return jnp.logaddexp(jnp.log(jnp.maximum(lb, LB_FLOOR)), jnp.log1p(-lb) + jax.nn.log_sigmoid(zf))


def hgrn_scan(q, v, log_f, s0):
    bsz, t_len, heads, _ = q.shape
    n_chunks = t_len // A_CHUNK

    def to_chunks(a):
        a = a.astype(jnp.float32).reshape(bsz, n_chunks, A_CHUNK, heads, a.shape[-1])
        return a.transpose(1, 0, 3, 2, 4)

    tri = jnp.tril(jnp.ones((A_CHUNK, A_CHUNK), dtype=bool))[:, :, None]

    def step(state, inp):
        qc, vc, lfc = inp
        kc = -jnp.expm1(lfc)
        b = jnp.cumsum(lfc, axis=2)
        o_inter = jnp.einsum('bhtk,bhkv->bhtv', qc * jnp.exp(b), state)
        diff = b[:, :, :, None, :] - b[:, :, None, :, :]
        decay = jnp.where(tri, jnp.exp(jnp.where(tri, diff, 0.0)), 0.0)
        scores = jnp.einsum('bhtk,bhsk,bhtsk->bhts', qc, kc, decay)
        o_intra = jnp.einsum('bhts,bhsv->bhtv', scores, vc)
        b_end = b[:, :, -1:, :]
        new_state = jnp.exp(b_end)[:, :, 0, :, None] * state + jnp.einsum('bhsk,bhsv->bhkv', kc * jnp.exp(b_end - b), vc)
        return new_state, o_inter + o_intra

    final_state, o = lax.scan(step, s0, (to_chunks(q), to_chunks(v), to_chunks(log_f)))
    o = o.transpose(1, 0, 3, 2, 4).reshape(bsz, t_len, heads, -1)
    return o, final_state


def hgrn_scan_inputs(z_a, lb_f, lb_b):
    bsz, t_len, _ = z_a.shape
    q, ff, fb, i = jnp.split(z_a, 4, axis=-1)
    heads = lambda a: a.reshape(bsz, t_len, A_HEADS, A_HEAD_DIM)
    return heads(jax.nn.silu(q)), log_forget(heads(ff), lb_f), log_forget(heads(fb), lb_b), heads(i)


def hgrn_bidir(q, lf_f, lf_b, v, s0_f, s0_b):
    flip = lambda a: jnp.flip(a, axis=1)
    o_f, s_f = hgrn_scan(q, v, lf_f, s0_f)
    o_b, s_b = hgrn_scan(flip(q), flip(v), flip(lf_b), s0_b)
    return o_f + flip(o_b), s_f, s_b


def fourier_mix(z):
    bsz, t_len, _ = z.shape
    zg = z.astype(jnp.float32).reshape(bsz, t_len, B_GROUPS, B_GROUP_DIM)
    out = jnp.fft.fft2(zg, axes=(1, 3), norm='ortho').real
    return out.reshape(bsz, t_len, B_WIDTH).astype(z.dtype)


def spatial_gate(u, v, w_s, b_s, norm_w, n_chunks):
    bsz, t_len, _ = u.shape
    u = jax.nn.gelu(u, approximate=False)
    v = rmsnorm(jax.nn.gelu(v, approximate=False).reshape(bsz, t_len, C_GROUPS, C_GROUP_DIM), norm_w)
    vc = v.reshape(bsz, n_chunks, SGU_CHUNK, C_GROUPS, C_GROUP_DIM)
    mixed = jnp.einsum('gpq,bnqgd->bnpgd', w_s, vc) + b_s.T[None, None, :, :, None]
    return u * mixed.reshape(bsz, t_len, C_WIDTH)


def token_mix(h, s0_f, s0_b, n_chunks, w_in, lb_f, lb_b, gnorm_w, sgu_w, sgu_b, sgu_norm_w, w_branch, w_out):
    bsz, t_len, _ = h.shape
    z = h @ w_in
    splits = np.cumsum([A_IN, A_WIDTH, B_WIDTH, C_WIDTH, C_WIDTH]).tolist()
    z_a, g_a, z_b, u, v, gates = jnp.split(z, splits, axis=-1)
    q, lf_f, lf_b, vv = hgrn_scan_inputs(z_a, lb_f, lb_b)
    o_a, s_f, s_b = hgrn_bidir(q, lf_f, lf_b, vv, s0_f, s0_b)
    o_a = rmsnorm(o_a, gnorm_w).reshape(bsz, t_len, A_WIDTH).astype(h.dtype) * jax.nn.silu(g_a)
    o_b = fourier_mix(z_b)
    o_c = spatial_gate(u, v, sgu_w, sgu_b, sgu_norm_w, n_chunks)
    g = jax.nn.sigmoid(gates).reshape(bsz, t_len, N_BRANCH, D_MODEL)
    merged = g[:, :, 0] * (o_a @ w_branch[0]) + g[:, :, 1] * (o_b @ w_branch[1]) + g[:, :, 2] * (o_c @ w_branch[2])
    return merged @ w_out, s_f, s_b


def context_states(h, w_in, lb_f, lb_b, s0):
    q, lf_f, lf_b, vv = hgrn_scan_inputs(h @ w_in[:, :A_IN], lb_f, lb_b)
    _, s_f, s_b = hgrn_bidir(q, lf_f, lf_b, vv, s0, s0)
    return s_f, s_b


def expert_choice_moe(h, router_w, w13, w2):
    bsz, t_len, d = h.shape
    cap = CAPACITY_FACTOR * t_len // N_EXPERTS
    affinity = jax.nn.softmax((h @ router_w).astype(jnp.float32), axis=-1)
    gate, idx = lax.top_k(affinity.transpose(0, 2, 1), cap)
    xe = jax.vmap(lambda hb, ib: hb[ib])(h, idx)
    a, b = jnp.split(jnp.einsum('becd,edf->becf', xe, w13), 2, axis=-1)
    y = jnp.einsum('becf,efd->becd', jax.nn.silu(a) * b, w2) * gate[..., None].astype(h.dtype)
    return jax.vmap(lambda ib, yb: jnp.zeros((t_len, d), yb.dtype).at[ib.reshape(-1)].add(yb.reshape(-1, d)))(idx, y)


def setup_inputs(seed: int = 0) -> dict:
    key = jax.random.key(seed)
    ks = jax.random.split(key, 18)
    nrm = lambda k, shape, s: jax.random.normal(k, shape, jnp.float32) * s
    return {
        'x': nrm(ks[0], (BATCH, SEQ, D_MODEL), 1.0),
        'c': nrm(ks[1], (BATCH, D_MODEL), 1.0),
        'ctx': nrm(ks[2], (BATCH, CTX_LEN, D_MODEL), 1.0),
        'c_ctx': nrm(ks[3], (D_MODEL,), 1.0),
        'ada_w': nrm(ks[4], (DEPTH, D_MODEL, 6 * D_MODEL), 0.5 * D_MODEL ** -0.5),
        'ada_b': nrm(ks[5], (DEPTH, 6 * D_MODEL), 0.02),
        'norm_w': 1.0 + nrm(ks[6], (DEPTH, 4, D_MODEL), 0.05),
        'w_in': nrm(ks[7], (DEPTH, D_MODEL, IN_COLS), D_MODEL ** -0.5),
        'hgrn_lb_raw': nrm(ks[8], (2, DEPTH, A_WIDTH), 0.5),
        'hgrn_gnorm_w': 1.0 + nrm(ks[9], (DEPTH, A_HEAD_DIM), 0.05),
        'sgu_w': nrm(ks[10], (DEPTH, C_GROUPS, SGU_CHUNK, SGU_CHUNK), SGU_CHUNK ** -0.5),
        'sgu_b': 1.0 + nrm(ks[11], (DEPTH, C_GROUPS, SGU_CHUNK), 0.05),
        'sgu_norm_w': 1.0 + nrm(ks[12], (DEPTH, C_GROUPS, C_GROUP_DIM), 0.05),
        'w_branch': nrm(ks[13], (DEPTH, N_BRANCH, A_WIDTH, D_MODEL), A_WIDTH ** -0.5),
        'w_out': nrm(ks[14], (DEPTH, D_MODEL, D_MODEL), D_MODEL ** -0.5),
        'router_w': nrm(ks[15], (DEPTH, D_MODEL, N_EXPERTS), D_MODEL ** -0.5),
        'exp_w13': nrm(ks[16], (DEPTH, N_EXPERTS, D_MODEL, 2 * D_EXPERT), D_MODEL ** -0.5),
        'exp_w2': nrm(ks[17], (DEPTH, N_EXPERTS, D_EXPERT, D_MODEL), D_EXPERT ** -0.5),
    }


def reference(x, c, ctx, c_ctx, ada_w, ada_b, norm_w, w_in, hgrn_lb_raw, hgrn_gnorm_w, sgu_w, sgu_b, sgu_norm_w, w_branch, w_out, router_w, exp_w13, exp_w2):
    rows = x.shape[1] // GRID_W
    lat_chunks = rows // ROWS_PER_CHUNK
    ctx_chunks = ctx.shape[1] // SGU_CHUNK
    p = jax.nn.softmax(hgrn_lb_raw.astype(jnp.float32), axis=1)
    lower_bounds = jnp.cumsum(p, axis=1) - p[:, :1]
    zero_state = jnp.zeros((x.shape[0], A_HEADS, A_HEAD_DIM, A_HEAD_DIM), jnp.float32)
    for l in range(DEPTH):
        last = l == DEPTH - 1
        lb_f = lower_bounds[0, l].reshape(A_HEADS, A_HEAD_DIM)
        lb_b = lower_bounds[1, l].reshape(A_HEADS, A_HEAD_DIM)
        sh1, sc1, g1, sh2, sc2, g2 = adaln(c[:, None, :], ada_w[l], ada_b[l])
        csh1, csc1, cg1, csh2, csc2, cg2 = adaln(c_ctx[None, None, :], ada_w[l], ada_b[l])
        mix_params = (w_in[l], lb_f, lb_b, hgrn_gnorm_w[l], sgu_w[l], sgu_b[l], sgu_norm_w[l], w_branch[l], w_out[l])
        h_ctx = modulate(rmsnorm(ctx, norm_w[l, 0]), csh1, csc1)
        h_lat = modulate(rmsnorm(x, norm_w[l, 0]), sh1, sc1)
        if last:
            s_f, s_b = context_states(h_ctx, w_in[l], lb_f, lb_b, zero_state)
        else:
            y_ctx, s_f, s_b = token_mix(h_ctx, zero_state, zero_state, ctx_chunks, *mix_params)
            ctx_mid = ctx + cg1 * rmsnorm(y_ctx, norm_w[l, 1])
        y_lat, _, _ = token_mix(h_lat, s_f, s_b, lat_chunks, *mix_params)
        x = x + g1 * rmsnorm(y_lat, norm_w[l, 1])
        m_lat = expert_choice_moe(modulate(rmsnorm(x, norm_w[l, 2]), sh2, sc2), router_w[l], exp_w13[l], exp_w2[l])
        x = x + g2 * rmsnorm(m_lat, norm_w[l, 3])
        if not last:
            m_ctx = expert_choice_moe(modulate(rmsnorm(ctx_mid, norm_w[l, 2]), csh2, csc2), router_w[l], exp_w13[l], exp_w2[l])
            ctx = ctx_mid + cg2 * rmsnorm(m_ctx, norm_w[l, 3])
    return x
```

```python
import functools
import math

import numpy as np
import jax
import jax.numpy as jnp
from jax import lax
from jax.experimental import pallas as pl
from jax.experimental.pallas import tpu as pltpu

F32 = jnp.float32
BF16 = jnp.bfloat16

D_MODEL = 1024
HEADS = 4
HEAD_DIM = 128
WIDTH = HEADS * HEAD_DIM
SGU_CHUNK = 128
N_BRANCH = 3
N_EXPERTS = 16
CAPACITY_FACTOR = 2
EPS = 1e-6
LB_FLOOR = 1e-20

COL_Q, COL_FF, COL_FB, COL_I = 0, WIDTH, 2 * WIDTH, 3 * WIDTH
COL_GA = 4 * WIDTH
COL_FOURIER = 5 * WIDTH
COL_U = 6 * WIDTH
COL_V = 7 * WIDTH
COL_GATES = 8 * WIDTH
IN_COLS = COL_GATES + N_BRANCH * D_MODEL
A_IN = 4 * WIDTH

HGRN_CHUNK = 64
HGRN_SUB = 16
EXP_CLAMP = 80.0

VMEM_LIMIT = 56 * 1024 * 1024
MOD_ROWS = 16


def _cparams(*sem):
    return pltpu.CompilerParams(dimension_semantics=sem, vmem_limit_bytes=VMEM_LIMIT)


def _sigmoid(v):
    return 1.0 / (1.0 + jnp.exp(-v))


def _rms(v, w):
    return v * lax.rsqrt(jnp.mean(v * v, axis=-1, keepdims=True) + EPS) * w


def _gelu(v):
    return 0.5 * v * (1.0 + lax.erf(v * (1.0 / math.sqrt(2.0))))


def _adaln_kernel(c_ref, w_ref, b_ref, o_ref):
    c = c_ref[...]
    s = c * _sigmoid(c)
    o_ref[0] = jnp.dot(s, w_ref[0], precision=lax.Precision.HIGHEST, preferred_element_type=F32) + b_ref[0]


def _adaln(cvec, ada_w, ada_b):
    depth, d, n = ada_w.shape
    tn = 512
    return pl.pallas_call(
        _adaln_kernel,
        out_shape=jax.ShapeDtypeStruct((depth, MOD_ROWS, n), F32),
        grid=(depth, n // tn),
        in_specs=[
            pl.BlockSpec((MOD_ROWS, d), lambda l, j: (0, 0)),
            pl.BlockSpec((1, d, tn), lambda l, j: (l, 0, j)),
            pl.BlockSpec((1, 1, tn), lambda l, j: (l, 0, j)),
        ],
        out_specs=pl.BlockSpec((1, MOD_ROWS, tn), lambda l, j: (l, 0, j)),
        compiler_params=_cparams("arbitrary", "arbitrary"),
        name="adaln",
    )(cvec, ada_w, ada_b.reshape(depth, 1, n))


def _inproj_kernel(x_ref, nw_ref, sh_ref, sc_ref, w_ref, z_ref, h_scr):
    @pl.when(pl.program_id(2) == 0)
    def _():
        h = _rms(x_ref[0], nw_ref[...]) * (1.0 + sc_ref[0]) + sh_ref[0]
        h_scr[...] = h.astype(BF16)

    z_ref[0] = jnp.dot(h_scr[...], w_ref[...], preferred_element_type=F32).astype(z_ref.dtype)


def _inproj(x, nw, sh, sc, w):
    bsz, t, d = x.shape
    n = w.shape[1]
    tm = min(t, 1024)
    tn = 1024
    return pl.pallas_call(
        _inproj_kernel,
        out_shape=jax.ShapeDtypeStruct((bsz, t, n), BF16),
        grid=(bsz, t // tm, n // tn),
        in_specs=[
            pl.BlockSpec((1, tm, d), lambda b, i, j: (b, i, 0)),
            pl.BlockSpec((1, d), lambda b, i, j: (0, 0)),
            pl.BlockSpec((1, 1, d), lambda b, i, j: (b, 0, 0)),
            pl.BlockSpec((1, 1, d), lambda b, i, j: (b, 0, 0)),
            pl.BlockSpec((d, tn), lambda b, i, j: (0, j)),
        ],
        out_specs=pl.BlockSpec((1, tm, tn), lambda b, i, j: (b, i, j)),
        scratch_shapes=[pltpu.VMEM((tm, d), BF16)],
        compiler_params=_cparams("arbitrary", "arbitrary", "arbitrary"),
        name="inproj",
    )(x, nw.reshape(1, d), sh, sc, w)


def _lower_bound(raw, layer):
    m = jnp.max(raw, axis=0, keepdims=True)
    e = jnp.exp(raw - m)
    p = e / jnp.sum(e, axis=0, keepdims=True)
    lb = jnp.zeros_like(p[0:1])
    for j in range(1, layer + 1):
        lb = lb + p[j:j + 1]
    return lb


def _hgrn_chunk(zq, zf, zi, lb, st_ref, reverse):
    c = zq.shape[0]
    zq = zq.astype(F32)
    zf = zf.astype(F32)
    q = zq * _sigmoid(zq)
    e = jnp.exp(-jnp.clip(zf, -EXP_CLAMP, EXP_CLAMP))
    r = 1.0 / (1.0 + e)
    lbm = jnp.maximum(lb, LB_FLOOR)
    f = lbm + (1.0 - lb) * r
    kk = (1.0 - lb) * (e * r) - (lbm - lb)
    lf = jnp.where(f < 0.5, jnp.log(f), jnp.log1p(-kk))
    v = zi.astype(BF16)

    row = lax.broadcasted_iota(jnp.int32, (c, c), 0)
    col = lax.broadcasted_iota(jnp.int32, (c, c), 1)
    tri = (col >= row) if reverse else (col <= row)
    trib = jnp.where(tri, 1.0, 0.0).astype(BF16)
    l1 = lf.astype(BF16)
    d1 = lf - l1.astype(F32)
    l2 = d1.astype(BF16)
    l3 = (d1 - l2.astype(F32)).astype(BF16)
    b = (jnp.dot(trib, l1, preferred_element_type=F32) + jnp.dot(trib, l2, preferred_element_type=F32)
         + jnp.dot(trib, l3, preferred_element_type=F32))
    b_excl = b - lf
    b_end = b[0:1] if reverse else b[c - 1:c]
    qe = (q * jnp.exp(b)).astype(BF16)
    kd = (kk * jnp.exp(b_end - b)).astype(BF16)
    g_end = jnp.exp(b_end)

    sub = lax.broadcasted_iota(jnp.int32, (c, 1), 0) // HGRN_SUB
    qa, ka = [], []
    for j in range(c // HGRN_SUB):
        first = j * HGRN_SUB + (HGRN_SUB - 1 if reverse else 0)
        rj = b_excl[first:first + 1]
        qa.append((q * jnp.exp(jnp.minimum(b - rj, 0.0))).astype(BF16))
        kj = kk * jnp.exp(jnp.minimum(rj - b, EXP_CLAMP))
        ka.append(jnp.where(sub == j, kj, 0.0).astype(BF16))

    outs = []
    for h in range(HEADS):
        sl = slice(h * HEAD_DIM, (h + 1) * HEAD_DIM)
        q_aug = jnp.concatenate([a[:, sl] for a in qa], axis=1)
        k_aug = jnp.concatenate([a[:, sl] for a in ka], axis=1)
        scores = lax.dot_general(q_aug, k_aug, (((1,), (1,)), ((), ())), preferred_element_type=F32)
        scores = jnp.where(tri, scores, 0.0).astype(BF16)
        st = st_ref[h]
        o = jnp.dot(scores, v[:, sl], preferred_element_type=F32)
        o = o + lax.dot_general(qe[:, sl], st.astype(BF16), (((1,), (1,)), ((), ())), preferred_element_type=F32)
        upd = lax.dot_general(v[:, sl], kd[:, sl], (((0,), (0,)), ((), ())), preferred_element_type=F32)
        st_ref[h] = st * g_end[:, sl] + upd
        outs.append(o)
    return jnp.concatenate(outs, axis=1)


def _hgrn_kernel(zqf, zff, zif, zqb, zfb, zib, lbf_ref, lbb_ref, s0f, s0b,
                 of_ref, ob_ref, sf_ref, sb_ref, st_scr, *, layer, nsteps):
    n = pl.program_id(1)

    @pl.when(n == 0)
    def _():
        st_scr[0] = s0f[0]
        st_scr[1] = s0b[0]

    lb_f = _lower_bound(lbf_ref[...], layer)
    lb_b = _lower_bound(lbb_ref[...], layer)
    of_ref[0] = _hgrn_chunk(zqf[0], zff[0], zif[0], lb_f, st_scr.at[0], False).astype(of_ref.dtype)
    ob_ref[0] = _hgrn_chunk(zqb[0], zfb[0], zib[0], lb_b, st_scr.at[1], True).astype(ob_ref.dtype)

    @pl.when(n == nsteps - 1)
    def _():
        sf_ref[0] = st_scr[0]
        sb_ref[0] = st_scr[1]


def _hgrn(z, lb_raw, s0f, s0b, layer):
    bsz, t, _ = z.shape
    c = HGRN_CHUNK
    n = t // c
    depth = lb_raw.shape[1]
    zspec = lambda colblk, rev: pl.BlockSpec(
        (1, c, WIDTH), (lambda b, i: (b, n - 1 - i, colblk)) if rev else (lambda b, i: (b, i, colblk)))
    sspec = pl.BlockSpec((1, HEADS, HEAD_DIM, HEAD_DIM), lambda b, i: (b, 0, 0, 0))
    ospec = lambda rev: pl.BlockSpec((1, c, WIDTH), (lambda b, i: (b, n - 1 - i, 0)) if rev else (lambda b, i: (b, i, 0)))
    st_shape = jax.ShapeDtypeStruct((bsz, HEADS, HEAD_DIM, HEAD_DIM), F32)
    return pl.pallas_call(
        functools.partial(_hgrn_kernel, layer=layer, nsteps=n),
        out_shape=(jax.ShapeDtypeStruct((bsz, t, WIDTH), BF16), jax.ShapeDtypeStruct((bsz, t, WIDTH), BF16),
                   st_shape, st_shape),
        grid=(bsz, n),
        in_specs=[
            zspec(COL_Q // WIDTH, False), zspec(COL_FF // WIDTH, False), zspec(COL_I // WIDTH, False),
            zspec(COL_Q // WIDTH, True), zspec(COL_FB // WIDTH, True), zspec(COL_I // WIDTH, True),
            pl.BlockSpec((depth, WIDTH), lambda b, i: (0, 0)),
            pl.BlockSpec((depth, WIDTH), lambda b, i: (0, 0)),
            sspec, sspec,
        ],
        out_specs=(ospec(False), ospec(True), sspec, sspec),
        scratch_shapes=[pltpu.VMEM((2, HEADS, HEAD_DIM, HEAD_DIM), F32)],
        compiler_params=_cparams("arbitrary", "arbitrary"),
        name="hgrn",
    )(z, z, z, z, z, z, lb_raw[0], lb_raw[1], s0f, s0b)


def _fourier_kernel(z_ref, dft_ref, cd_ref, sd_ref, o_ref, r_scr, *, t, scale):
    @pl.when(pl.program_id(1) == 0)
    def _():
        for g in range(HEADS):
            zg = z_ref[0, :, g * HEAD_DIM:(g + 1) * HEAD_DIM]
            r_scr[0:t, g * HEAD_DIM:(g + 1) * HEAD_DIM] = jnp.dot(
                zg, cd_ref[...], preferred_element_type=F32).astype(BF16)
            r_scr[t:2 * t, g * HEAD_DIM:(g + 1) * HEAD_DIM] = (-jnp.dot(
                zg, sd_ref[...], preferred_element_type=F32)).astype(BF16)

    o_ref[0] = (jnp.dot(dft_ref[...], r_scr[...], preferred_element_type=F32) * scale).astype(o_ref.dtype)


def _fourier(z, dft, cd, sd):
    bsz, t, _ = z.shape
    tp = min(t, 512)
    return pl.pallas_call(
        functools.partial(_fourier_kernel, t=t, scale=1.0 / math.sqrt(t * HEAD_DIM)),
        out_shape=jax.ShapeDtypeStruct((bsz, t, WIDTH), BF16),
        grid=(bsz, t // tp),
        in_specs=[
            pl.BlockSpec((1, t, WIDTH), lambda b, i: (b, 0, COL_FOURIER // WIDTH)),
            pl.BlockSpec((tp, 2 * t), lambda b, i: (i, 0)),
            pl.BlockSpec((HEAD_DIM, HEAD_DIM), lambda b, i: (0, 0)),
            pl.BlockSpec((HEAD_DIM, HEAD_DIM), lambda b, i: (0, 0)),
        ],
        out_specs=pl.BlockSpec((1, tp, WIDTH), lambda b, i: (b, i, 0)),
        scratch_shapes=[pltpu.VMEM((2 * t, WIDTH), BF16)],
        compiler_params=_cparams("arbitrary", "arbitrary"),
        name="fourier",
    )(z, dft, cd, sd)


def _dft_tables(t):
    r = int(round(math.sqrt(t)))
    assert r * r == t
    s = np.arange(t, dtype=np.int64)
    a_ang = 2.0 * np.pi * ((np.arange(r)[:, None] * r * s[None, :]) % t) / t
    b_ang = 2.0 * np.pi * ((np.arange(r)[:, None] * s[None, :]) % t) / t
    ac, asn = jnp.asarray(np.cos(a_ang), F32)[:, None, :], jnp.asarray(np.sin(a_ang), F32)[:, None, :]
    bc, bsn = jnp.asarray(np.cos(b_ang), F32)[None, :, :], jnp.asarray(np.sin(b_ang), F32)[None, :, :]
    ct = (ac * bc - asn * bsn).reshape(t, t)
    st = (asn * bc + ac * bsn).reshape(t, t)
    return jnp.concatenate([ct, st], axis=1).astype(BF16)


def _channel_dft():
    d = np.arange(HEAD_DIM)
    ang = 2.0 * np.pi * ((d[:, None] * d[None, :]) % HEAD_DIM) / HEAD_DIM
    return jnp.asarray(np.cos(ang), F32).astype(BF16), jnp.asarray(np.sin(ang), F32).astype(BF16)


def _merge_kernel(hf_ref, hb_ref, ga_ref, fo_ref, u_ref, v_ref, g0_ref, g1_ref, g2_ref, x_ref,
                  gnw_ref, sw_ref, sb_ref, snw_ref, wb_ref, wo_ref, nw1_ref, nw2_ref,
                  gate1_ref, sh2_ref, sc2_ref, rw_ref,
                  xo_ref, h2_ref, aff_ref):
    tm = x_ref.shape[1]
    hsum = hf_ref[0].astype(F32) + hb_ref[0].astype(F32)
    ga = ga_ref[0].astype(F32)
    oa = []
    for h in range(HEADS):
        sl = slice(h * HEAD_DIM, (h + 1) * HEAD_DIM)
        oa.append(_rms(hsum[:, sl], gnw_ref[...]))
    o_a = (jnp.concatenate(oa, axis=1) * (ga * _sigmoid(ga))).astype(BF16)
    u = _gelu(u_ref[0].astype(F32))
    vv = _gelu(v_ref[0].astype(F32))
    rows = []
    for ch in range(tm // SGU_CHUNK):
        rs = slice(ch * SGU_CHUNK, (ch + 1) * SGU_CHUNK)
        cols = []
        for g in range(HEADS):
            sl = slice(g * HEAD_DIM, (g + 1) * HEAD_DIM)
            vn = _rms(vv[rs, sl], snw_ref[g:g + 1, :]).astype(BF16)
            cols.append(jnp.dot(sw_ref[g], vn, preferred_element_type=F32) + sb_ref[g])
        rows.append(jnp.concatenate(cols, axis=1))
    o_c = (u * jnp.concatenate(rows, axis=0)).astype(BF16)
    merged = _sigmoid(g0_ref[0].astype(F32)) * jnp.dot(o_a, wb_ref[0], preferred_element_type=F32)
    merged += _sigmoid(g1_ref[0].astype(F32)) * jnp.dot(fo_ref[0], wb_ref[1], preferred_element_type=F32)
    merged += _sigmoid(g2_ref[0].astype(F32)) * jnp.dot(o_c, wb_ref[2], preferred_element_type=F32)
    y = jnp.dot(merged.astype(BF16), wo_ref[...], preferred_element_type=F32)
    xn = x_ref[0] + gate1_ref[0] * _rms(y, nw1_ref[...])
    xo_ref[0] = xn
    h2 = _rms(xn, nw2_ref[...]) * (1.0 + sc2_ref[0]) + sh2_ref[0]
    h2_ref[0] = h2
    logits = lax.dot_general(rw_ref[...], h2, (((1,), (1,)), ((), ())),
                             precision=lax.Precision.HIGHEST, preferred_element_type=F32)
    ex = jnp.exp(logits - jnp.max(logits, axis=0, keepdims=True))
    aff_ref[0] = ex / jnp.sum(ex, axis=0, keepdims=True)


def _merge(z, hf, hb, fo, x, gnw, sgu_w, sgu_bias, snw, wb, wo, nw1, nw2, gate1, sh2, sc2, rw_t):
    bsz, t, d = x.shape
    tm = 256
    zw = lambda colblk: pl.BlockSpec((1, tm, WIDTH), lambda b, i: (b, i, colblk))
    zd = lambda colblk: pl.BlockSpec((1, tm, d), lambda b, i: (b, i, colblk))
    tok = lambda width: pl.BlockSpec((1, tm, width), lambda b, i: (b, i, 0))
    full = lambda *shape: pl.BlockSpec(shape, lambda b, i: (0,) * len(shape))
    mod = pl.BlockSpec((1, 1, d), lambda b, i: (b, 0, 0))
    return pl.pallas_call(
        _merge_kernel,
        out_shape=(jax.ShapeDtypeStruct((bsz, t, d), F32), jax.ShapeDtypeStruct((bsz, t, d), F32),
                   jax.ShapeDtypeStruct((bsz, N_EXPERTS, t), F32)),
        grid=(bsz, t // tm),
        in_specs=[
            tok(WIDTH), tok(WIDTH), zw(COL_GA // WIDTH), tok(WIDTH), zw(COL_U // WIDTH), zw(COL_V // WIDTH),
            zd(COL_GATES // d), zd(COL_GATES // d + 1), zd(COL_GATES // d + 2), tok(d),
            full(1, HEAD_DIM), full(HEADS, SGU_CHUNK, SGU_CHUNK), full(HEADS, SGU_CHUNK, HEAD_DIM),
            full(HEADS, HEAD_DIM), full(N_BRANCH, WIDTH, d), full(d, d), full(1, d), full(1, d),
            mod, mod, mod, full(N_EXPERTS, d),
        ],
        out_specs=(tok(d), tok(d), pl.BlockSpec((1, N_EXPERTS, tm), lambda b, i: (b, 0, i))),
        compiler_params=_cparams("arbitrary", "arbitrary"),
        name="merge",
    )(hf, hb, z, fo, z, z, z, z, z, x, gnw, sgu_w, sgu_bias, snw, wb, wo, nw1, nw2, gate1, sh2, sc2, rw_t)


def _prefix_count(mask, t):
    lanes = 128
    s_i = lax.broadcasted_iota(jnp.int32, (lanes, lanes), 0)
    t_i = lax.broadcasted_iota(jnp.int32, (lanes, lanes), 1)
    upper = jnp.where(s_i < t_i, 1.0, 0.0).astype(BF16)
    off = jnp.zeros((mask.shape[0], 1), F32)
    parts = []
    for blk in range(t // lanes):
        m = mask[:, blk * lanes:(blk + 1) * lanes]
        parts.append(jnp.dot(m.astype(BF16), upper, preferred_element_type=F32) + off)
        off = off + jnp.sum(m, axis=1, keepdims=True)
    return jnp.concatenate(parts, axis=1)


def _select_kernel(aff_ref, idx_ref, gate_ref, *, t, cap):
    a = aff_ref[0]
    ne = a.shape[0]

    def search(_, carry):
        lo, hi = carry
        mid = lo + ((hi - lo) >> 1)
        cnt = jnp.sum(jnp.where(a >= pltpu.bitcast(mid, F32), 1.0, 0.0), axis=1, keepdims=True)
        ok = cnt >= cap
        return jnp.where(ok, mid, lo), jnp.where(ok, hi, mid)

    lo0 = jnp.zeros((ne, 1), jnp.int32)
    hi0 = jnp.full((ne, 1), 0x7F800000, jnp.int32)
    thr, _ = lax.fori_loop(0, 31, search, (lo0, hi0))
    ge = jnp.where(a >= pltpu.bitcast(thr, F32), 1.0, 0.0)
    gt = jnp.where(a >= pltpu.bitcast(thr + 1, F32), 1.0, 0.0)
    eq = ge - gt
    need = cap - jnp.sum(gt, axis=1, keepdims=True)
    sel = gt + eq * jnp.where(_prefix_count(eq, t) < need, 1.0, 0.0)
    pos = jnp.where(sel > 0.0, _prefix_count(sel, t), -1.0)

    tok = lax.broadcasted_iota(jnp.int32, (1, t), 1)
    a1 = a.astype(BF16)
    d1 = a - a1.astype(F32)
    a2 = d1.astype(BF16)
    a3 = (d1 - a2.astype(F32)).astype(BF16)
    rid = lax.broadcasted_iota(jnp.int32, (8, t), 0)
    tile = min(cap, 128)
    for e in range(ne):
        lhs = jnp.where(rid == 0, (tok >> 6).astype(F32),
              jnp.where(rid == 1, (tok & 63).astype(F32),
              jnp.where(rid == 2, a1[e:e + 1].astype(F32),
              jnp.where(rid == 3, a2[e:e + 1].astype(F32),
              jnp.where(rid == 4, a3[e:e + 1].astype(F32), 0.0))))).astype(BF16)
        for r0 in range(0, cap, tile):
            slot = (lax.broadcasted_iota(jnp.int32, (tile, 1), 0) + r0).astype(F32)
            onehot = jnp.where(pos[e:e + 1] == slot, 1.0, 0.0).astype(BF16)
            res = lax.dot_general(lhs, onehot, (((1,), (1,)), ((), ())), preferred_element_type=F32)
            idx_ref[0, e:e + 1, r0:r0 + tile] = (res[0:1] * 64.0 + res[1:2]).astype(jnp.int32)
            gate_ref[0, e:e + 1, r0:r0 + tile] = res[2:3] + res[3:4] + res[4:5]


def _select(aff, cap):
    bsz, ne, t = aff.shape
    return pl.pallas_call(
        functools.partial(_select_kernel, t=t, cap=cap),
        out_shape=(jax.ShapeDtypeStruct((bsz, ne, cap), jnp.int32), jax.ShapeDtypeStruct((bsz, ne, cap), F32)),
        grid=(bsz,),
        in_specs=[pl.BlockSpec((1, ne, t), lambda b: (b, 0, 0))],
        out_specs=(pl.BlockSpec((1, ne, cap), lambda b: (b, 0, 0)), pl.BlockSpec((1, ne, cap), lambda b: (b, 0, 0))),
        compiler_params=_cparams("arbitrary"),
        name="select",
    )(aff)


def _experts_kernel(idx_ref, idxn_ref, h_hbm, w13_ref, w2_ref, y_ref, xe, sem, *, bsz, t, cap, nsteps):
    e = pl.program_id(0)
    b = pl.program_id(1)
    step = e * bsz + b
    slot = step % 2

    def row_copy(src_row, dst_slot, j):
        return pltpu.make_async_copy(h_hbm.at[pl.ds(src_row, 1), :], xe.at[dst_slot, pl.ds(j, 1), :], sem.at[dst_slot])

    def issue(iref, batch, dst_slot):
        def body(j, carry):
            row_copy(batch * t + iref[0, 0, j], dst_slot, j).start()
            return carry
        lax.fori_loop(0, cap, body, 0, unroll=8)

    @pl.when(step == 0)
    def _():
        issue(idx_ref, b, slot)

    @pl.when(step + 1 < nsteps)
    def _():
        issue(idxn_ref, (step + 1) % bsz, 1 - slot)

    def wait_body(j, carry):
        row_copy(0, slot, j).wait()
        return carry
    lax.fori_loop(0, cap, wait_body, 0, unroll=8)

    f = w2_ref.shape[1]
    xb = xe[slot].astype(BF16)
    ab = jnp.dot(xb, w13_ref[0], preferred_element_type=F32)
    a = ab[:, :f]
    hid = (a * _sigmoid(a) * ab[:, f:]).astype(BF16)
    y_ref[0, 0] = jnp.dot(hid, w2_ref[0], preferred_element_type=F32)


def _experts(h2, idx, w13, w2):
    bsz, t, d = h2.shape
    ne, cap = idx.shape[1], idx.shape[2]
    f = w2.shape[1]
    nsteps = ne * bsz
    idx3 = idx.reshape(bsz * ne, 1, cap)

    def nxt(e, b):
        s = jnp.minimum(e * bsz + b + 1, nsteps - 1)
        return ((s % bsz) * ne + s // bsz, 0, 0)

    return pl.pallas_call(
        functools.partial(_experts_kernel, bsz=bsz, t=t, cap=cap, nsteps=nsteps),
        out_shape=jax.ShapeDtypeStruct((bsz, ne, cap, d), F32),
        grid=(ne, bsz),
        in_specs=[
            pl.BlockSpec((1, 1, cap), lambda e, b: (b * ne + e, 0, 0), memory_space=pltpu.SMEM),
            pl.BlockSpec((1, 1, cap), nxt, memory_space=pltpu.SMEM),
            pl.BlockSpec(memory_space=pl.ANY),
            pl.BlockSpec((1, d, 2 * f), lambda e, b: (e, 0, 0)),
            pl.BlockSpec((1, f, d), lambda e, b: (e, 0, 0)),
        ],
        out_specs=pl.BlockSpec((1, 1, cap, d), lambda e, b: (b, e, 0, 0)),
        scratch_shapes=[pltpu.VMEM((2, cap, d), F32), pltpu.SemaphoreType.DMA((2,))],
        compiler_params=_cparams("arbitrary", "arbitrary"),
        name="experts",
    )(idx3, idx3, h2.reshape(bsz * t, d), w13, w2)


def _combine_kernel(idx_ref, gate_ref, y_ref, m_ref, *, cap):
    @pl.when(pl.program_id(2) == 0)
    def _():
        m_ref[...] = jnp.zeros_like(m_ref)

    def body(j, carry):
        i = idx_ref[0, 0, j]
        m_ref[0, pl.ds(i, 1), :] += gate_ref[0, 0, j] * y_ref[0, 0, pl.ds(j, 1), :]
        return carry
    lax.fori_loop(0, cap, body, 0, unroll=8)


def _combine(y, idx, gate, t):
    bsz, ne, cap, d = y.shape
    dt = 512
    idx3 = idx.reshape(bsz * ne, 1, cap)
    gate3 = gate.reshape(bsz * ne, 1, cap)
    sspec = pl.BlockSpec((1, 1, cap), lambda b, j, e: (b * ne + e, 0, 0), memory_space=pltpu.SMEM)
    return pl.pallas_call(
        functools.partial(_combine_kernel, cap=cap),
        out_shape=jax.ShapeDtypeStruct((bsz, t, d), F32),
        grid=(bsz, d // dt, ne),
        in_specs=[sspec, sspec, pl.BlockSpec((1, 1, cap, dt), lambda b, j, e: (b, e, 0, j))],
        out_specs=pl.BlockSpec((1, t, dt), lambda b, j, e: (b, 0, j)),
        compiler_params=_cparams("arbitrary", "arbitrary", "arbitrary"),
        name="combine",
    )(idx3, gate3, y)


def _residual_kernel(x_ref, m_ref, g_ref, nw_ref, o_ref):
    o_ref[0] = x_ref[0] + g_ref[0] * _rms(m_ref[0], nw_ref[...])


def _residual(x, m, gate, nw):
    bsz, t, d = x.shape
    tm = min(t, 512)
    tok = pl.BlockSpec((1, tm, d), lambda b, i: (b, i, 0))
    return pl.pallas_call(
        _residual_kernel,
        out_shape=jax.ShapeDtypeStruct((bsz, t, d), F32),
        grid=(bsz, t // tm),
        in_specs=[tok, tok, pl.BlockSpec((1, 1, d), lambda b, i: (b, 0, 0)), pl.BlockSpec((1, d), lambda b, i: (0, 0))],
        out_specs=tok,
        compiler_params=_cparams("arbitrary", "arbitrary"),
        name="residual",
    )(x, m, gate, nw.reshape(1, d))


def _token_mix_and_moe(stream, z, hf, hb, dft, cdsd, mods, lw):
    sh2, sc2, g1, g2 = mods
    t = stream.shape[1]
    fo = _fourier(z, dft, *cdsd)
    mid, h2, aff = _merge(z, hf, hb, fo, stream, lw["gnw"], lw["sgu_w"], lw["sgu_bias"], lw["snw"], lw["wb"],
                          lw["wo"], lw["nw1"], lw["nw2"], g1, sh2, sc2, lw["rw_t"])
    cap = CAPACITY_FACTOR * t // N_EXPERTS
    idx, gate = _select(aff, cap)
    y = _experts(h2, idx, lw["w13"], lw["w2"])
    m = _combine(y, idx, gate, t)
    return _residual(mid, m, g2, lw["nw3"])


def kernel(x, c, ctx, c_ctx, ada_w, ada_b, norm_w, w_in, hgrn_lb_raw, hgrn_gnorm_w, sgu_w, sgu_b, sgu_norm_w,
           w_branch, w_out, router_w, exp_w13, exp_w2):
    bsz, t, d = x.shape
    tc = ctx.shape[1]
    depth = ada_w.shape[0]
    assert d == D_MODEL and bsz + 1 <= MOD_ROWS and w_in.shape[2] == IN_COLS

    cvec = jnp.zeros((MOD_ROWS, d), F32).at[:bsz].set(c).at[bsz].set(c_ctx)
    mod = _adaln(cvec, ada_w, ada_b).reshape(depth, MOD_ROWS, 6, d)
    dft_lat, dft_ctx = _dft_tables(t), _dft_tables(tc)
    cdsd = _channel_dft()
    zero_state = jnp.zeros((bsz, HEADS, HEAD_DIM, HEAD_DIM), F32)

    for l in range(depth):
        last = l == depth - 1
        lat = [mod[l, :bsz, k][:, None, :] for k in range(6)]
        cm = [jnp.broadcast_to(mod[l, bsz, k][None, None, :], (bsz, 1, d)) for k in range(6)]
        w_in_l = w_in[l].astype(BF16)
        lw = dict(
            gnw=hgrn_gnorm_w[l].reshape(1, HEAD_DIM),
            sgu_w=sgu_w[l].astype(BF16),
            sgu_bias=jnp.broadcast_to(sgu_b[l][:, :, None], (HEADS, SGU_CHUNK, HEAD_DIM)),
            snw=sgu_norm_w[l],
            wb=w_branch[l].astype(BF16),
            wo=w_out[l].astype(BF16),
            nw1=norm_w[l, 1].reshape(1, d), nw2=norm_w[l, 2].reshape(1, d), nw3=norm_w[l, 3],
            rw_t=router_w[l].T,
            w13=exp_w13[l].astype(BF16), w2=exp_w2[l].astype(BF16),
        )
        lb_raw = hgrn_lb_raw[:, :, :]
        z_ctx = _inproj(ctx, norm_w[l, 0], cm[0], cm[1], w_in_l[:, :A_IN] if last else w_in_l)
        hf_c, hb_c, s_f, s_b = _hgrn(z_ctx, lb_raw, zero_state, zero_state, l)
        z_lat = _inproj(x, norm_w[l, 0], lat[0], lat[1], w_in_l)
        hf, hb, _, _ = _hgrn(z_lat, lb_raw, s_f, s_b, l)
        x = _token_mix_and_moe(x, z_lat, hf, hb, dft_lat, cdsd, (lat[3], lat[4], lat[2], lat[5]), lw)
        if not last:
            ctx = _token_mix_and_moe(ctx, z_ctx, hf_c, hb_c, dft_ctx, cdsd, (cm[3], cm[4], cm[2], cm[5]), lw)
    return x
```

```python
import functools
import math

import numpy as np
import jax
import jax.numpy as jnp
from jax import lax
from jax.experimental import pallas as pl
from jax.experimental.pallas import tpu as pltpu

F32 = jnp.float32
BF16 = jnp.bfloat16

D_MODEL = 1024
HEADS = 4
HEAD_DIM = 128
WIDTH = HEADS * HEAD_DIM
SGU_CHUNK = 128
N_BRANCH = 3
N_EXPERTS = 16
CAPACITY_FACTOR = 2
EPS = 1e-6
LB_FLOOR = 1e-20

COL_Q, COL_FF, COL_FB, COL_I = 0, WIDTH, 2 * WIDTH, 3 * WIDTH
COL_GA = 4 * WIDTH
COL_FOURIER = 5 * WIDTH
COL_U = 6 * WIDTH
COL_V = 7 * WIDTH
COL_GATES = 8 * WIDTH
IN_COLS = COL_GATES + N_BRANCH * D_MODEL
A_IN = 4 * WIDTH

HGRN_CHUNK = 64
HGRN_SUB = 16
HGRN_BLOCK = 256
EXP_CLAMP = 80.0
LOG2E = 1.4426950408889634
EXP2_CLAMP = 115.0

VMEM_LIMIT = 56 * 1024 * 1024
EXPERT_FCHUNK = 256
TILE_ROWS, LANES = 8, 128
MOD_ROWS = 16


def _cparams(*sem):
    return pltpu.CompilerParams(dimension_semantics=sem, vmem_limit_bytes=VMEM_LIMIT)


def _sigmoid(v):
    return 1.0 / (1.0 + jnp.exp(-v))


def _rms(v, w):
    return v * lax.rsqrt(jnp.mean(v * v, axis=-1, keepdims=True) + EPS) * w


def _gelu(v):
    return 0.5 * v * (1.0 + lax.erf(v * (1.0 / math.sqrt(2.0))))


def _store_token_tiles(ref, val):
    rows = val.shape[0]
    for k in range(TILE_ROWS):
        ref[pl.ds(k, rows, stride=TILE_ROWS), :] = val[:, k * LANES:(k + 1) * LANES]


def _load_token_tiles(ref, rows):
    return jnp.concatenate([ref[pl.ds(k, rows, stride=TILE_ROWS), :] for k in range(TILE_ROWS)], axis=1)


def _adaln_kernel(c_ref, w_ref, b_ref, o_ref):
    c = c_ref[...]
    s = c * _sigmoid(c)
    o_ref[0] = jnp.dot(s, w_ref[0], precision=lax.Precision.HIGHEST, preferred_element_type=F32) + b_ref[0]


def _adaln(cvec, ada_w, ada_b):
    depth, d, n = ada_w.shape
    tn = 512
    return pl.pallas_call(
        _adaln_kernel,
        out_shape=jax.ShapeDtypeStruct((depth, MOD_ROWS, n), F32),
        grid=(depth, n // tn),
        in_specs=[
            pl.BlockSpec((MOD_ROWS, d), lambda l, j: (0, 0)),
            pl.BlockSpec((1, d, tn), lambda l, j: (l, 0, j)),
            pl.BlockSpec((1, 1, tn), lambda l, j: (l, 0, j)),
        ],
        out_specs=pl.BlockSpec((1, MOD_ROWS, tn), lambda l, j: (l, 0, j)),
        compiler_params=_cparams("arbitrary", "arbitrary"),
        name="adaln",
    )(cvec, ada_w, ada_b.reshape(depth, 1, n))


def _inproj_kernel(x_ref, nw_ref, sh_ref, sc_ref, w_ref, z_ref, *, tn):
    h = (_rms(x_ref[0], nw_ref[...]) * (1.0 + sc_ref[0]) + sh_ref[0]).astype(BF16)
    for j in range(w_ref.shape[1] // tn):
        z_ref[0, :, j * tn:(j + 1) * tn] = jnp.dot(
            h, w_ref[:, j * tn:(j + 1) * tn], preferred_element_type=F32).astype(z_ref.dtype)


def _inproj(x, nw, sh, sc, w):
    bsz, t, d = x.shape
    n = w.shape[1]
    tm = min(t, 512)
    return pl.pallas_call(
        functools.partial(_inproj_kernel, tn=1024),
        out_shape=jax.ShapeDtypeStruct((bsz, t, n), BF16),
        grid=(bsz, t // tm),
        in_specs=[
            pl.BlockSpec((1, tm, d), lambda b, i: (b, i, 0)),
            pl.BlockSpec((1, d), lambda b, i: (0, 0)),
            pl.BlockSpec((1, 1, d), lambda b, i: (b, 0, 0)),
            pl.BlockSpec((1, 1, d), lambda b, i: (b, 0, 0)),
            pl.BlockSpec((d, n), lambda b, i: (0, 0), pipeline_mode=pl.Buffered(1)),
        ],
        out_specs=pl.BlockSpec((1, tm, n), lambda b, i: (b, i, 0)),
        compiler_params=_cparams("arbitrary", "arbitrary"),
        name="inproj",
    )(x, nw.reshape(1, d), sh, sc, w)


def _lower_bound(raw, layer):
    m = jnp.max(raw, axis=0, keepdims=True)
    e = jnp.exp(raw - m)
    p = e / jnp.sum(e, axis=0, keepdims=True)
    lb = jnp.zeros_like(p[0:1])
    for j in range(1, layer + 1):
        lb = lb + p[j:j + 1]
    return lb


def _hgrn_block(zq, zf, zi, lb, st_ref, reverse):
    tb = zq.shape[0]
    c, sc = HGRN_CHUNK, HGRN_SUB
    ns = c // sc
    zq = zq.astype(F32)
    zf = zf.astype(F32)
    q = zq * _sigmoid(zq)
    e = jnp.exp2(jnp.maximum(zf, -EXP_CLAMP) * (-LOG2E))
    r = 1.0 / (1.0 + e)
    lbm = jnp.maximum(lb, LB_FLOOR)
    f = lbm + (1.0 - lb) * r
    kk = (1.0 - lb) * (e * r) - (lbm - lb)
    lf = jnp.log2(f)
    v = zi.astype(BF16)

    row = lax.broadcasted_iota(jnp.int32, (tb, tb), 0)
    col = lax.broadcasted_iota(jnp.int32, (tb, tb), 1)
    absorbed = (col >= row) if reverse else (col <= row)
    trib = jnp.where(absorbed & ((row // c) == (col // c)), 1.0, 0.0).astype(BF16)
    l1 = lf.astype(BF16)
    l2 = (lf - l1.astype(F32)).astype(BF16)
    b = jnp.dot(trib, l1, preferred_element_type=F32) + jnp.dot(trib, l2, preferred_element_type=F32)
    b_excl = b - lf

    refs = []
    for s in range(tb // sc):
        first = s * sc + (sc - 1 if reverse else 0)
        refs.append(b_excl[first:first + 1])
    r_own = jnp.concatenate([jnp.broadcast_to(x, (sc, WIDTH)) for x in refs], axis=0)
    q_own = q * jnp.exp2(b - r_own)
    k_own = (kk * jnp.exp2(jnp.minimum(r_own - b, EXP2_CLAMP))).astype(BF16)
    zero_tile = jnp.zeros((sc, HEAD_DIM), BF16)
    ti = lax.broadcasted_iota(jnp.int32, (c, c), 0)
    si = lax.broadcasted_iota(jnp.int32, (c, c), 1)
    tri_c = (si >= ti) if reverse else (si <= ti)

    nch = tb // c
    order = list(range(nch - 1, -1, -1)) if reverse else list(range(nch))
    qe_c, g_c, sc_c, v_c, kd_c = {}, {}, {}, {}, {}
    for ch in order:
        lo = ch * c
        b_c = b[lo:lo + c]
        b_end = b_c[0:1] if reverse else b_c[c - 1:c]
        qe_c[ch] = (q[lo:lo + c] * jnp.exp2(b_c)).astype(BF16)
        kd_c[ch] = (kk[lo:lo + c] * jnp.exp2(b_end - b_c)).astype(BF16)
        g_c[ch] = jnp.exp2(b_end)
        v_c[ch] = v[lo:lo + c]
        qij = {}
        for i in range(ns):
            for j in range(ns):
                if (j >= i) if reverse else (j <= i):
                    gi = jnp.exp2(refs[ch * ns + i] - refs[ch * ns + j])
                    qij[i, j] = (q_own[lo + i * sc:lo + (i + 1) * sc] * gi).astype(BF16)
        for h in range(HEADS):
            sl = slice(h * HEAD_DIM, (h + 1) * HEAD_DIM)
            q_aug = jnp.concatenate([
                jnp.concatenate([qij[i, j][:, sl] if (i, j) in qij else zero_tile for j in range(ns)], axis=1)
                for i in range(ns)], axis=0)
            k_aug = jnp.concatenate([
                jnp.concatenate([k_own[lo + i * sc:lo + (i + 1) * sc, sl] if j == i else zero_tile
                                 for j in range(ns)], axis=1)
                for i in range(ns)], axis=0)
            scores = lax.dot_general(q_aug, k_aug, (((1,), (1,)), ((), ())), preferred_element_type=F32)
            sc_c[ch, h] = jnp.where(tri_c, scores, 0.0).astype(BF16)
    o_c, upd_c = {}, {}
    for ch in order:
        for h in range(HEADS):
            sl = slice(h * HEAD_DIM, (h + 1) * HEAD_DIM)
            o_c[ch, h] = jnp.dot(sc_c[ch, h], v_c[ch][:, sl], preferred_element_type=F32)
            upd_c[ch, h] = lax.dot_general(v_c[ch][:, sl], kd_c[ch][:, sl], (((0,), (0,)), ((), ())),
                                           preferred_element_type=F32)
    for h in range(HEADS):
        sl = slice(h * HEAD_DIM, (h + 1) * HEAD_DIM)
        st = st_ref[h]
        for ch in order:
            o_c[ch, h] = o_c[ch, h] + lax.dot_general(qe_c[ch][:, sl], st.astype(BF16), (((1,), (1,)), ((), ())),
                                                      preferred_element_type=F32)
            st = st * g_c[ch][:, sl] + upd_c[ch, h]
        st_ref[h] = st
    return jnp.concatenate([jnp.concatenate([o_c[ch, h] for h in range(HEADS)], axis=1) for ch in range(nch)], axis=0)


def _hgrn_kernel(zqf, zff, zif, zqb, zfb, zib, lbf_ref, lbb_ref, s0f, s0b,
                 of_ref, ob_ref, sf_ref, sb_ref, st_scr, *, layer, nsteps):
    n = pl.program_id(1)

    @pl.when(n == 0)
    def _():
        st_scr[0] = s0f[0]
        st_scr[1] = s0b[0]

    lb_f = _lower_bound(lbf_ref[...], layer)
    lb_b = _lower_bound(lbb_ref[...], layer)
    of_ref[0] = _hgrn_block(zqf[0], zff[0], zif[0], lb_f, st_scr.at[0], False).astype(of_ref.dtype)
    ob_ref[0] = _hgrn_block(zqb[0], zfb[0], zib[0], lb_b, st_scr.at[1], True).astype(ob_ref.dtype)

    @pl.when(n == nsteps - 1)
    def _():
        sf_ref[0] = st_scr[0]
        sb_ref[0] = st_scr[1]


def _hgrn(z, lb_raw, s0f, s0b, layer):
    bsz, t, _ = z.shape
    c = min(t, HGRN_BLOCK)
    n = t // c
    depth = lb_raw.shape[1]
    zspec = lambda colblk, rev: pl.BlockSpec(
        (1, c, WIDTH), (lambda b, i: (b, n - 1 - i, colblk)) if rev else (lambda b, i: (b, i, colblk)))
    sspec = pl.BlockSpec((1, HEADS, HEAD_DIM, HEAD_DIM), lambda b, i: (b, 0, 0, 0))
    ospec = lambda rev: pl.BlockSpec((1, c, WIDTH), (lambda b, i: (b, n - 1 - i, 0)) if rev else (lambda b, i: (b, i, 0)))
    st_shape = jax.ShapeDtypeStruct((bsz, HEADS, HEAD_DIM, HEAD_DIM), F32)
    return pl.pallas_call(
        functools.partial(_hgrn_kernel, layer=layer, nsteps=n),
        out_shape=(jax.ShapeDtypeStruct((bsz, t, WIDTH), BF16), jax.ShapeDtypeStruct((bsz, t, WIDTH), BF16),
                   st_shape, st_shape),
        grid=(bsz, n),
        in_specs=[
            zspec(COL_Q // WIDTH, False), zspec(COL_FF // WIDTH, False), zspec(COL_I // WIDTH, False),
            zspec(COL_Q // WIDTH, True), zspec(COL_FB // WIDTH, True), zspec(COL_I // WIDTH, True),
            pl.BlockSpec((depth, WIDTH), lambda b, i: (0, 0)),
            pl.BlockSpec((depth, WIDTH), lambda b, i: (0, 0)),
            sspec, sspec,
        ],
        out_specs=(ospec(False), ospec(True), sspec, sspec),
        scratch_shapes=[pltpu.VMEM((2, HEADS, HEAD_DIM, HEAD_DIM), F32)],
        compiler_params=_cparams("arbitrary", "arbitrary"),
        name="hgrn",
    )(z, z, z, z, z, z, lb_raw[0], lb_raw[1], s0f, s0b)


def _fourier_kernel(z_ref, dft_ref, cd_ref, sd_ref, o_ref, r_scr, *, t, scale):
    @pl.when(pl.program_id(1) == 0)
    def _():
        for g in range(HEADS):
            zg = z_ref[0, :, g * HEAD_DIM:(g + 1) * HEAD_DIM]
            r_scr[0:t, g * HEAD_DIM:(g + 1) * HEAD_DIM] = jnp.dot(
                zg, cd_ref[...], preferred_element_type=F32).astype(BF16)
            r_scr[t:2 * t, g * HEAD_DIM:(g + 1) * HEAD_DIM] = (-jnp.dot(
                zg, sd_ref[...], preferred_element_type=F32)).astype(BF16)

    o_ref[0] = (jnp.dot(dft_ref[...], r_scr[...], preferred_element_type=F32) * scale).astype(o_ref.dtype)


def _fourier(z, dft, cd, sd):
    bsz, t, _ = z.shape
    tp = min(t, 512)
    return pl.pallas_call(
        functools.partial(_fourier_kernel, t=t, scale=1.0 / math.sqrt(t * HEAD_DIM)),
        out_shape=jax.ShapeDtypeStruct((bsz, t, WIDTH), BF16),
        grid=(bsz, t // tp),
        in_specs=[
            pl.BlockSpec((1, t, WIDTH), lambda b, i: (b, 0, COL_FOURIER // WIDTH)),
            pl.BlockSpec((tp, 2 * t), lambda b, i: (i, 0)),
            pl.BlockSpec((HEAD_DIM, HEAD_DIM), lambda b, i: (0, 0)),
            pl.BlockSpec((HEAD_DIM, HEAD_DIM), lambda b, i: (0, 0)),
        ],
        out_specs=pl.BlockSpec((1, tp, WIDTH), lambda b, i: (b, i, 0)),
        scratch_shapes=[pltpu.VMEM((2 * t, WIDTH), BF16)],
        compiler_params=_cparams("arbitrary", "arbitrary"),
        name="fourier",
    )(z, dft, cd, sd)


def _dft_tables(t):
    r = int(round(math.sqrt(t)))
    assert r * r == t
    s = np.arange(t, dtype=np.int64)
    a_ang = 2.0 * np.pi * ((np.arange(r)[:, None] * r * s[None, :]) % t) / t
    b_ang = 2.0 * np.pi * ((np.arange(r)[:, None] * s[None, :]) % t) / t
    ac, asn = jnp.asarray(np.cos(a_ang), F32)[:, None, :], jnp.asarray(np.sin(a_ang), F32)[:, None, :]
    bc, bsn = jnp.asarray(np.cos(b_ang), F32)[None, :, :], jnp.asarray(np.sin(b_ang), F32)[None, :, :]
    ct = (ac * bc - asn * bsn).reshape(t, t)
    st = (asn * bc + ac * bsn).reshape(t, t)
    return jnp.concatenate([ct, st], axis=1).astype(BF16)


def _channel_dft():
    d = np.arange(HEAD_DIM)
    ang = 2.0 * np.pi * ((d[:, None] * d[None, :]) % HEAD_DIM) / HEAD_DIM
    return jnp.asarray(np.cos(ang), F32).astype(BF16), jnp.asarray(np.sin(ang), F32).astype(BF16)


def _merge_kernel(hf_ref, hb_ref, ga_ref, fo_ref, u_ref, v_ref, g0_ref, g1_ref, g2_ref, x_ref,
                  gnw_ref, sw_ref, sb_ref, snw_ref, wb_ref, wo_ref, nw1_ref, nw2_ref,
                  gate1_ref, sh2_ref, sc2_ref, rw_ref,
                  xo_ref, h2_ref, aff_ref):
    tm = x_ref.shape[1]
    hsum = hf_ref[0].astype(F32) + hb_ref[0].astype(F32)
    ga = ga_ref[0].astype(F32)
    oa = []
    for h in range(HEADS):
        sl = slice(h * HEAD_DIM, (h + 1) * HEAD_DIM)
        oa.append(_rms(hsum[:, sl], gnw_ref[...]))
    o_a = (jnp.concatenate(oa, axis=1) * (ga * _sigmoid(ga))).astype(BF16)
    u = _gelu(u_ref[0].astype(F32))
    vv = _gelu(v_ref[0].astype(F32))
    rows = []
    for ch in range(tm // SGU_CHUNK):
        rs = slice(ch * SGU_CHUNK, (ch + 1) * SGU_CHUNK)
        cols = []
        for g in range(HEADS):
            sl = slice(g * HEAD_DIM, (g + 1) * HEAD_DIM)
            vn = _rms(vv[rs, sl], snw_ref[g:g + 1, :]).astype(BF16)
            cols.append(jnp.dot(sw_ref[g], vn, preferred_element_type=F32) + sb_ref[g])
        rows.append(jnp.concatenate(cols, axis=1))
    o_c = (u * jnp.concatenate(rows, axis=0)).astype(BF16)
    merged = _sigmoid(g0_ref[0].astype(F32)) * jnp.dot(o_a, wb_ref[0], preferred_element_type=F32)
    merged += _sigmoid(g1_ref[0].astype(F32)) * jnp.dot(fo_ref[0], wb_ref[1], preferred_element_type=F32)
    merged += _sigmoid(g2_ref[0].astype(F32)) * jnp.dot(o_c, wb_ref[2], preferred_element_type=F32)
    y = jnp.dot(merged.astype(BF16), wo_ref[...], preferred_element_type=F32)
    xn = x_ref[0] + gate1_ref[0] * _rms(y, nw1_ref[...])
    xo_ref[0] = xn
    h2 = _rms(xn, nw2_ref[...]) * (1.0 + sc2_ref[0]) + sh2_ref[0]
    _store_token_tiles(h2_ref.at[0], h2)
    logits = lax.dot_general(rw_ref[...], h2, (((1,), (1,)), ((), ())),
                             precision=lax.Precision.HIGHEST, preferred_element_type=F32)
    ex = jnp.exp(logits - jnp.max(logits, axis=0, keepdims=True))
    aff_ref[0] = ex / jnp.sum(ex, axis=0, keepdims=True)


def _merge(z, hf, hb, fo, x, gnw, sgu_w, sgu_bias, snw, wb, wo, nw1, nw2, gate1, sh2, sc2, rw_t):
    bsz, t, d = x.shape
    tm = min(t, 512)
    zw = lambda colblk: pl.BlockSpec((1, tm, WIDTH), lambda b, i: (b, i, colblk))
    zd = lambda colblk: pl.BlockSpec((1, tm, d), lambda b, i: (b, i, colblk))
    tok = lambda width: pl.BlockSpec((1, tm, width), lambda b, i: (b, i, 0))
    full = lambda *shape: pl.BlockSpec(shape, lambda b, i: (0,) * len(shape))
    mod = pl.BlockSpec((1, 1, d), lambda b, i: (b, 0, 0))
    return pl.pallas_call(
        _merge_kernel,
        out_shape=(jax.ShapeDtypeStruct((bsz, t, d), F32), jax.ShapeDtypeStruct((bsz, t * TILE_ROWS, LANES), F32),
                   jax.ShapeDtypeStruct((bsz, N_EXPERTS, t), F32)),
        grid=(bsz, t // tm),
        in_specs=[
            tok(WIDTH), tok(WIDTH), zw(COL_GA // WIDTH), tok(WIDTH), zw(COL_U // WIDTH), zw(COL_V // WIDTH),
            zd(COL_GATES // d), zd(COL_GATES // d + 1), zd(COL_GATES // d + 2), tok(d),
            full(1, HEAD_DIM), full(HEADS, SGU_CHUNK, SGU_CHUNK), full(HEADS, SGU_CHUNK, HEAD_DIM),
            full(HEADS, HEAD_DIM), full(N_BRANCH, WIDTH, d), full(d, d), full(1, d), full(1, d),
            mod, mod, mod, full(N_EXPERTS, d),
        ],
        out_specs=(tok(d), pl.BlockSpec((1, tm * TILE_ROWS, LANES), lambda b, i: (b, i, 0)),
                   pl.BlockSpec((1, N_EXPERTS, tm), lambda b, i: (b, 0, i))),
        compiler_params=_cparams("arbitrary", "arbitrary"),
        name="merge",
    )(hf, hb, z, fo, z, z, z, z, z, x, gnw, sgu_w, sgu_bias, snw, wb, wo, nw1, nw2, gate1, sh2, sc2, rw_t)


def _prefix_count(mask, t):
    lanes = 128
    s_i = lax.broadcasted_iota(jnp.int32, (lanes, lanes), 0)
    t_i = lax.broadcasted_iota(jnp.int32, (lanes, lanes), 1)
    upper = jnp.where(s_i < t_i, 1.0, 0.0).astype(BF16)
    off = jnp.zeros((mask.shape[0], 1), F32)
    parts = []
    for blk in range(t // lanes):
        m = mask[:, blk * lanes:(blk + 1) * lanes]
        parts.append(jnp.dot(m.astype(BF16), upper, preferred_element_type=F32) + off)
        off = off + jnp.sum(m, axis=1, keepdims=True)
    return jnp.concatenate(parts, axis=1)


def _select_kernel(aff_ref, idx_ref, gate_ref, *, t, cap):
    a = aff_ref[0]
    ne = a.shape[0]

    def search(_, carry):
        lo, hi = carry
        mid = lo + ((hi - lo) >> 1)
        cnt = jnp.sum(jnp.where(a >= pltpu.bitcast(mid, F32), 1.0, 0.0), axis=1, keepdims=True)
        ok = cnt >= cap
        return jnp.where(ok, mid, lo), jnp.where(ok, hi, mid)

    lo0 = jnp.zeros((ne, 1), jnp.int32)
    hi0 = jnp.full((ne, 1), 0x7F800000, jnp.int32)
    thr, _ = lax.fori_loop(0, 31, search, (lo0, hi0))
    ge = jnp.where(a >= pltpu.bitcast(thr, F32), 1.0, 0.0)
    gt = jnp.where(a >= pltpu.bitcast(thr + 1, F32), 1.0, 0.0)
    eq = ge - gt
    need = cap - jnp.sum(gt, axis=1, keepdims=True)
    sel = gt + eq * jnp.where(_prefix_count(eq, t) < need, 1.0, 0.0)
    pos = jnp.where(sel > 0.0, _prefix_count(sel, t), -1.0)

    tok = lax.broadcasted_iota(jnp.int32, (1, t), 1)
    a1 = a.astype(BF16)
    d1 = a - a1.astype(F32)
    a2 = d1.astype(BF16)
    a3 = (d1 - a2.astype(F32)).astype(BF16)
    rid = lax.broadcasted_iota(jnp.int32, (8, t), 0)
    tile = min(cap, 128)
    for e in range(ne):
        lhs = jnp.where(rid == 0, (tok >> 6).astype(F32),
              jnp.where(rid == 1, (tok & 63).astype(F32),
              jnp.where(rid == 2, a1[e:e + 1].astype(F32),
              jnp.where(rid == 3, a2[e:e + 1].astype(F32),
              jnp.where(rid == 4, a3[e:e + 1].astype(F32), 0.0))))).astype(BF16)
        for r0 in range(0, cap, tile):
            slot = (lax.broadcasted_iota(jnp.int32, (tile, 1), 0) + r0).astype(F32)
            onehot = jnp.where(pos[e:e + 1] == slot, 1.0, 0.0).astype(BF16)
            res = lax.dot_general(lhs, onehot, (((1,), (1,)), ((), ())), preferred_element_type=F32)
            idx_ref[0, e:e + 1, r0:r0 + tile] = (res[0:1] * 64.0 + res[1:2]).astype(jnp.int32)
            gate_ref[0, e:e + 1, r0:r0 + tile] = res[2:3] + res[3:4] + res[4:5]


def _select(aff, cap):
    bsz, ne, t = aff.shape
    return pl.pallas_call(
        functools.partial(_select_kernel, t=t, cap=cap),
        out_shape=(jax.ShapeDtypeStruct((bsz, ne, cap), jnp.int32), jax.ShapeDtypeStruct((bsz, ne, cap), F32)),
        grid=(bsz,),
        in_specs=[pl.BlockSpec((1, ne, t), lambda b: (b, 0, 0))],
        out_specs=(pl.BlockSpec((1, ne, cap), lambda b: (b, 0, 0)), pl.BlockSpec((1, ne, cap), lambda b: (b, 0, 0))),
        compiler_params=_cparams("arbitrary"),
        name="select",
    )(aff)


def _experts_kernel(idx_ref, idxn_ref, h_hbm, w13_ref, w2_ref, y_ref, xe, sem, fence, *, bsz, t, cap, nsteps):
    e = pl.program_id(0)
    b = pl.program_id(1)
    step = e * bsz + b
    slot = step % 2
    f = w2_ref.shape[1]
    fc = min(f, EXPERT_FCHUNK)
    nck = f // fc
    rows_per = cap // nck

    def row_copy(src_row, dst_slot, j):
        return pltpu.make_async_copy(h_hbm.at[pl.ds(pl.multiple_of(src_row * TILE_ROWS, TILE_ROWS), TILE_ROWS), :],
                                     xe.at[dst_slot, pl.ds(j * TILE_ROWS, TILE_ROWS), :], sem.at[dst_slot])

    @pl.when(step == 0)
    def _():
        def body(j, carry):
            row_copy(b * t + idx_ref[0, 0, j], slot, j).start()
            return carry
        lax.fori_loop(0, cap, body, 0, unroll=8)

    def wait_body(j, carry):
        row_copy(0, slot, j).wait()
        return carry
    lax.fori_loop(0, cap, wait_body, 0, unroll=8)

    last = step + 1 >= nsteps
    nbase = jnp.where(last, b, (step + 1) % bsz) * t
    xb = _load_token_tiles(xe.at[slot], cap).astype(BF16)
    y = None
    for k in range(nck):
        a = jnp.dot(xb, w13_ref[0, :, k * fc:(k + 1) * fc], preferred_element_type=F32)
        g = jnp.dot(xb, w13_ref[0, :, f + k * fc:f + (k + 1) * fc], preferred_element_type=F32)
        hid = (a * _sigmoid(a) * g).astype(BF16)
        yk = jnp.dot(hid, w2_ref[0, k * fc:(k + 1) * fc, :], preferred_element_type=F32)
        y = yk if y is None else y + yk
        for j in range(k * rows_per, (k + 1) * rows_per):
            row_copy(nbase + idxn_ref[0, 0, j], 1 - slot, j).start()
        pl.semaphore_signal(fence, 1)
        pl.semaphore_wait(fence, 1)
    _store_token_tiles(y_ref.at[0, 0], y)

    @pl.when(last)
    def _():
        def drain(j, carry):
            row_copy(0, 1 - slot, j).wait()
            return carry
        lax.fori_loop(0, cap, drain, 0, unroll=8)


def _experts(h2, idx, w13, w2):
    bsz = h2.shape[0]
    t = h2.shape[1] // TILE_ROWS
    d = D_MODEL
    ne, cap = idx.shape[1], idx.shape[2]
    f = w2.shape[1]
    nsteps = ne * bsz
    idx3 = idx.reshape(bsz * ne, 1, cap)

    def nxt(e, b):
        s = jnp.minimum(e * bsz + b + 1, nsteps - 1)
        return ((s % bsz) * ne + s // bsz, 0, 0)

    return pl.pallas_call(
        functools.partial(_experts_kernel, bsz=bsz, t=t, cap=cap, nsteps=nsteps),
        out_shape=jax.ShapeDtypeStruct((bsz, ne, cap * TILE_ROWS, LANES), F32),
        grid=(ne, bsz),
        in_specs=[
            pl.BlockSpec((1, 1, cap), lambda e, b: (b * ne + e, 0, 0), memory_space=pltpu.SMEM),
            pl.BlockSpec((1, 1, cap), nxt, memory_space=pltpu.SMEM),
            pl.BlockSpec(memory_space=pl.ANY),
            pl.BlockSpec((1, d, 2 * f), lambda e, b: (e, 0, 0)),
            pl.BlockSpec((1, f, d), lambda e, b: (e, 0, 0)),
        ],
        out_specs=pl.BlockSpec((1, 1, cap * TILE_ROWS, LANES), lambda e, b: (b, e, 0, 0)),
        scratch_shapes=[pltpu.VMEM((2, cap * TILE_ROWS, LANES), F32), pltpu.SemaphoreType.DMA((2,)),
                        pltpu.SemaphoreType.REGULAR],
        compiler_params=_cparams("arbitrary", "arbitrary"),
        name="experts",
    )(idx3, idx3, h2.reshape(bsz * t * TILE_ROWS, LANES), w13, w2)


def _combine_kernel(idx_ref, gate_ref, y_ref, m_ref, *, cap):
    @pl.when(pl.program_id(1) == 0)
    def _():
        m_ref[...] = jnp.zeros_like(m_ref)

    group = 8

    def body(jj, carry):
        base = jj * group
        dst = [pl.ds(pl.multiple_of(idx_ref[0, 0, base + r] * TILE_ROWS, TILE_ROWS), TILE_ROWS)
               for r in range(group)]
        acc = [m_ref[0, d, :] for d in dst]
        for r in range(group):
            src = pl.ds(pl.multiple_of((base + r) * TILE_ROWS, TILE_ROWS), TILE_ROWS)
            m_ref[0, dst[r], :] = acc[r] + gate_ref[0, 0, base + r] * y_ref[0, 0, src, :]
        return carry
    lax.fori_loop(0, cap // group, body, 0, unroll=2)


def _combine(y, idx, gate, t):
    bsz, ne, cap = idx.shape
    idx3 = idx.reshape(bsz * ne, 1, cap)
    gate3 = gate.reshape(bsz * ne, 1, cap)
    sspec = pl.BlockSpec((1, 1, cap), lambda b, e: (b * ne + e, 0, 0), memory_space=pltpu.SMEM)
    return pl.pallas_call(
        functools.partial(_combine_kernel, cap=cap),
        out_shape=jax.ShapeDtypeStruct((bsz, t * TILE_ROWS, LANES), F32),
        grid=(bsz, ne),
        in_specs=[sspec, sspec, pl.BlockSpec((1, 1, cap * TILE_ROWS, LANES), lambda b, e: (b, e, 0, 0))],
        out_specs=pl.BlockSpec((1, t * TILE_ROWS, LANES), lambda b, e: (b, 0, 0)),
        compiler_params=_cparams("arbitrary", "arbitrary"),
        name="combine",
    )(idx3, gate3, y)


def _residual_kernel(x_ref, m_ref, g_ref, nw_ref, o_ref):
    m = _load_token_tiles(m_ref.at[0], x_ref.shape[1])
    o_ref[0] = x_ref[0] + g_ref[0] * _rms(m, nw_ref[...])


def _residual(x, m, gate, nw):
    bsz, t, d = x.shape
    tm = min(t, 512)
    tok = pl.BlockSpec((1, tm, d), lambda b, i: (b, i, 0))
    return pl.pallas_call(
        _residual_kernel,
        out_shape=jax.ShapeDtypeStruct((bsz, t, d), F32),
        grid=(bsz, t // tm),
        in_specs=[tok, pl.BlockSpec((1, tm * TILE_ROWS, LANES), lambda b, i: (b, i, 0)),
                  pl.BlockSpec((1, 1, d), lambda b, i: (b, 0, 0)), pl.BlockSpec((1, d), lambda b, i: (0, 0))],
        out_specs=tok,
        compiler_params=_cparams("arbitrary", "arbitrary"),
        name="residual",
    )(x, m, gate, nw.reshape(1, d))


def _token_mix_and_moe(stream, z, hf, hb, dft, cdsd, mods, lw):
    sh2, sc2, g1, g2 = mods
    t = stream.shape[1]
    fo = _fourier(z, dft, *cdsd)
    mid, h2, aff = _merge(z, hf, hb, fo, stream, lw["gnw"], lw["sgu_w"], lw["sgu_bias"], lw["snw"], lw["wb"],
                          lw["wo"], lw["nw1"], lw["nw2"], g1, sh2, sc2, lw["rw_t"])
    cap = CAPACITY_FACTOR * t // N_EXPERTS
    idx, gate = _select(aff, cap)
    y = _experts(h2, idx, lw["w13"], lw["w2"])
    m = _combine(y, idx, gate, t)
    return _residual(mid, m, g2, lw["nw3"])


def kernel(x, c, ctx, c_ctx, ada_w, ada_b, norm_w, w_in, hgrn_lb_raw, hgrn_gnorm_w, sgu_w, sgu_b, sgu_norm_w,
           w_branch, w_out, router_w, exp_w13, exp_w2):
    bsz, t, d = x.shape
    tc = ctx.shape[1]
    depth = ada_w.shape[0]
    assert d == D_MODEL and bsz + 1 <= MOD_ROWS and w_in.shape[2] == IN_COLS

    cvec = jnp.zeros((MOD_ROWS, d), F32).at[:bsz].set(c).at[bsz].set(c_ctx)
    mod = _adaln(cvec, ada_w, ada_b).reshape(depth, MOD_ROWS, 6, d)
    dft_lat, dft_ctx = _dft_tables(t), _dft_tables(tc)
    cdsd = _channel_dft()
    zero_state = jnp.zeros((bsz, HEADS, HEAD_DIM, HEAD_DIM), F32)

    for l in range(depth):
        last = l == depth - 1
        lat = [mod[l, :bsz, k][:, None, :] for k in range(6)]
        cm = [jnp.broadcast_to(mod[l, bsz, k][None, None, :], (bsz, 1, d)) for k in range(6)]
        w_in_l = w_in[l].astype(BF16)
        lw = dict(
            gnw=hgrn_gnorm_w[l].reshape(1, HEAD_DIM),
            sgu_w=sgu_w[l].astype(BF16),
            sgu_bias=jnp.broadcast_to(sgu_b[l][:, :, None], (HEADS, SGU_CHUNK, HEAD_DIM)),
            snw=sgu_norm_w[l],
            wb=w_branch[l].astype(BF16),
            wo=w_out[l].astype(BF16),
            nw1=norm_w[l, 1].reshape(1, d), nw2=norm_w[l, 2].reshape(1, d), nw3=norm_w[l, 3],
            rw_t=router_w[l].T,
            w13=exp_w13[l].astype(BF16), w2=exp_w2[l].astype(BF16),
        )
        lb_raw = hgrn_lb_raw[:, :, :]
        z_ctx = _inproj(ctx, norm_w[l, 0], cm[0], cm[1], w_in_l[:, :A_IN] if last else w_in_l)
        hf_c, hb_c, s_f, s_b = _hgrn(z_ctx, lb_raw, zero_state, zero_state, l)
        z_lat = _inproj(x, norm_w[l, 0], lat[0], lat[1], w_in_l)
        hf, hb, _, _ = _hgrn(z_lat, lb_raw, s_f, s_b, l)
        x = _token_mix_and_moe(x, z_lat, hf, hb, dft_lat, cdsd, (lat[3], lat[4], lat[2], lat[5]), lw)
        if not last:
            ctx = _token_mix_and_moe(ctx, z_ctx, hf_c, hb_c, dft_ctx, cdsd, (cm[3], cm[4], cm[2], cm[5]), lw)
    return x
```

```python
import functools
import math

import numpy as np
import jax
import jax.numpy as jnp
from jax import lax
from jax.experimental import pallas as pl
from jax.experimental.pallas import tpu as pltpu

F32 = jnp.float32
BF16 = jnp.bfloat16

D_MODEL = 1024
HEADS = 4
HEAD_DIM = 128
WIDTH = HEADS * HEAD_DIM
SGU_CHUNK = 128
N_BRANCH = 3
N_EXPERTS = 16
CAPACITY_FACTOR = 2
EPS = 1e-6
LB_FLOOR = 1e-20

COL_Q, COL_FF, COL_FB, COL_I = 0, WIDTH, 2 * WIDTH, 3 * WIDTH
COL_GA = 4 * WIDTH
COL_FOURIER = 5 * WIDTH
COL_U = 6 * WIDTH
COL_V = 7 * WIDTH
COL_GATES = 8 * WIDTH
IN_COLS = COL_GATES + N_BRANCH * D_MODEL
A_IN = 4 * WIDTH

HGRN_CHUNK = 64
HGRN_SUB = 16
HGRN_BLOCK = 256
EXP_CLAMP = 80.0
LOG2E = 1.4426950408889634
EXP2_CLAMP = 115.0

VMEM_LIMIT = 56 * 1024 * 1024
EXPERT_FCHUNK = 256
EXPERT_MIN_ROWS = 256
SLOT_COLS = 32
DFX_ROWS = 16
TILE_ROWS, LANES = 8, 128
MOD_ROWS = 16


def _cparams(*sem):
    return pltpu.CompilerParams(dimension_semantics=sem, vmem_limit_bytes=VMEM_LIMIT)


def _sigmoid(v):
    return 1.0 / (1.0 + jnp.exp(-v))


def _rms(v, w):
    return v * lax.rsqrt(jnp.mean(v * v, axis=-1, keepdims=True) + EPS) * w


def _gelu(v):
    return 0.5 * v * (1.0 + lax.erf(v * (1.0 / math.sqrt(2.0))))


def _store_token_tiles(ref, val):
    rows = val.shape[0]
    for k in range(TILE_ROWS):
        ref[pl.ds(k, rows, stride=TILE_ROWS), :] = val[:, k * LANES:(k + 1) * LANES]


def _load_token_tiles(ref, rows):
    return jnp.concatenate([ref[pl.ds(k, rows, stride=TILE_ROWS), :] for k in range(TILE_ROWS)], axis=1)


def _adaln_kernel(c_ref, w_ref, b_ref, o_ref):
    c = c_ref[...]
    s = c * _sigmoid(c)
    o_ref[0] = jnp.dot(s, w_ref[0], precision=lax.Precision.HIGHEST, preferred_element_type=F32) + b_ref[0]


def _adaln(cvec, ada_w, ada_b):
    depth, d, n = ada_w.shape
    tn = 512
    return pl.pallas_call(
        _adaln_kernel,
        out_shape=jax.ShapeDtypeStruct((depth, MOD_ROWS, n), F32),
        grid=(depth, n // tn),
        in_specs=[
            pl.BlockSpec((MOD_ROWS, d), lambda l, j: (0, 0)),
            pl.BlockSpec((1, d, tn), lambda l, j: (l, 0, j)),
            pl.BlockSpec((1, 1, tn), lambda l, j: (l, 0, j)),
        ],
        out_specs=pl.BlockSpec((1, MOD_ROWS, tn), lambda l, j: (l, 0, j)),
        compiler_params=_cparams("arbitrary", "arbitrary"),
        name="adaln",
    )(cvec, ada_w, ada_b.reshape(depth, 1, n))


def _inproj_kernel(x_ref, nw_ref, sh_ref, sc_ref, w_ref, z_ref, *, tn):
    h = (_rms(x_ref[0], nw_ref[...]) * (1.0 + sc_ref[0]) + sh_ref[0]).astype(BF16)
    for j in range(w_ref.shape[2] // tn):
        z_ref[0, :, j * tn:(j + 1) * tn] = jnp.dot(
            h, w_ref[0, :, j * tn:(j + 1) * tn], preferred_element_type=F32).astype(z_ref.dtype)


def _inproj(x, nw, sh, sc, w, layer, n):
    bsz, t, d = x.shape
    tm = min(t, 512)
    return pl.pallas_call(
        functools.partial(_inproj_kernel, tn=1024),
        out_shape=jax.ShapeDtypeStruct((bsz, t, n), BF16),
        grid=(bsz, t // tm),
        in_specs=[
            pl.BlockSpec((1, tm, d), lambda b, i: (b, i, 0)),
            pl.BlockSpec((1, d), lambda b, i: (0, 0)),
            pl.BlockSpec((1, 1, d), lambda b, i: (b, 0, 0)),
            pl.BlockSpec((1, 1, d), lambda b, i: (b, 0, 0)),
            pl.BlockSpec((1, d, n), lambda b, i: (layer, 0, 0), pipeline_mode=pl.Buffered(1)),
        ],
        out_specs=pl.BlockSpec((1, tm, n), lambda b, i: (b, i, 0)),
        compiler_params=_cparams("arbitrary", "arbitrary"),
        name="inproj",
    )(x, nw.reshape(1, d), sh, sc, w)


def _lower_bound(raw, layer):
    m = jnp.max(raw, axis=0, keepdims=True)
    e = jnp.exp(raw - m)
    p = e / jnp.sum(e, axis=0, keepdims=True)
    lb = jnp.zeros_like(p[0:1])
    for j in range(1, layer + 1):
        lb = lb + p[j:j + 1]
    return lb


def _hgrn_block(zq, zf, zi, lb, st_ref, reverse):
    tb = zq.shape[0]
    c, sc = HGRN_CHUNK, HGRN_SUB
    ns = c // sc
    zq = zq.astype(F32)
    zf = zf.astype(F32)
    q = zq * _sigmoid(zq)
    e = jnp.exp2(jnp.maximum(zf, -EXP_CLAMP) * (-LOG2E))
    r = 1.0 / (1.0 + e)
    lbm = jnp.maximum(lb, LB_FLOOR)
    f = lbm + (1.0 - lb) * r
    kk = (1.0 - lb) * (e * r) - (lbm - lb)
    lf = jnp.log2(f)
    v = zi.astype(BF16)

    row = lax.broadcasted_iota(jnp.int32, (tb, tb), 0)
    col = lax.broadcasted_iota(jnp.int32, (tb, tb), 1)
    absorbed = (col >= row) if reverse else (col <= row)
    trib = jnp.where(absorbed & ((row // c) == (col // c)), 1.0, 0.0).astype(BF16)
    l1 = lf.astype(BF16)
    l2 = (lf - l1.astype(F32)).astype(BF16)
    b = jnp.dot(trib, l1, preferred_element_type=F32) + jnp.dot(trib, l2, preferred_element_type=F32)
    b_excl = b - lf

    refs = []
    for s in range(tb // sc):
        first = s * sc + (sc - 1 if reverse else 0)
        refs.append(b_excl[first:first + 1])
    r_own = jnp.concatenate([jnp.broadcast_to(x, (sc, WIDTH)) for x in refs], axis=0)
    q_own = q * jnp.exp2(b - r_own)
    k_own = (kk * jnp.exp2(jnp.minimum(r_own - b, EXP2_CLAMP))).astype(BF16)
    zero_tile = jnp.zeros((sc, HEAD_DIM), BF16)
    ti = lax.broadcasted_iota(jnp.int32, (c, c), 0)
    si = lax.broadcasted_iota(jnp.int32, (c, c), 1)
    tri_c = (si >= ti) if reverse else (si <= ti)

    nch = tb // c
    order = list(range(nch - 1, -1, -1)) if reverse else list(range(nch))
    qe_c, g_c, sc_c, v_c, kd_c = {}, {}, {}, {}, {}
    for ch in order:
        lo = ch * c
        b_c = b[lo:lo + c]
        b_end = b_c[0:1] if reverse else b_c[c - 1:c]
        qe_c[ch] = (q[lo:lo + c] * jnp.exp2(b_c)).astype(BF16)
        kd_c[ch] = (kk[lo:lo + c] * jnp.exp2(b_end - b_c)).astype(BF16)
        g_c[ch] = jnp.exp2(b_end)
        v_c[ch] = v[lo:lo + c]
        qij = {}
        for i in range(ns):
            for j in range(ns):
                if (j >= i) if reverse else (j <= i):
                    gi = jnp.exp2(refs[ch * ns + i] - refs[ch * ns + j])
                    qij[i, j] = (q_own[lo + i * sc:lo + (i + 1) * sc] * gi).astype(BF16)
        for h in range(HEADS):
            sl = slice(h * HEAD_DIM, (h + 1) * HEAD_DIM)
            q_aug = jnp.concatenate([
                jnp.concatenate([qij[i, j][:, sl] if (i, j) in qij else zero_tile for j in range(ns)], axis=1)
                for i in range(ns)], axis=0)
            k_aug = jnp.concatenate([
                jnp.concatenate([k_own[lo + i * sc:lo + (i + 1) * sc, sl] if j == i else zero_tile
                                 for j in range(ns)], axis=1)
                for i in range(ns)], axis=0)
            scores = lax.dot_general(q_aug, k_aug, (((1,), (1,)), ((), ())), preferred_element_type=F32)
            sc_c[ch, h] = jnp.where(tri_c, scores, 0.0).astype(BF16)
    o_c, upd_c = {}, {}
    for ch in order:
        for h in range(HEADS):
            sl = slice(h * HEAD_DIM, (h + 1) * HEAD_DIM)
            o_c[ch, h] = jnp.dot(sc_c[ch, h], v_c[ch][:, sl], preferred_element_type=F32)
            upd_c[ch, h] = lax.dot_general(v_c[ch][:, sl], kd_c[ch][:, sl], (((0,), (0,)), ((), ())),
                                           preferred_element_type=F32)
    for h in range(HEADS):
        sl = slice(h * HEAD_DIM, (h + 1) * HEAD_DIM)
        st = st_ref[h]
        for ch in order:
            o_c[ch, h] = o_c[ch, h] + lax.dot_general(qe_c[ch][:, sl], st.astype(BF16), (((1,), (1,)), ((), ())),
                                                      preferred_element_type=F32)
            st = st * g_c[ch][:, sl] + upd_c[ch, h]
        st_ref[h] = st
    return jnp.concatenate([jnp.concatenate([o_c[ch, h] for h in range(HEADS)], axis=1) for ch in range(nch)], axis=0)


def _hgrn_kernel(zqf, zff, zif, zqb, zfb, zib, lbf_ref, lbb_ref, s0f, s0b,
                 of_ref, ob_ref, sf_ref, sb_ref, st_scr, *, layer, nsteps):
    n = pl.program_id(1)

    @pl.when(n == 0)
    def _():
        st_scr[0] = s0f[0]
        st_scr[1] = s0b[0]

    lb_f = _lower_bound(lbf_ref[...], layer)
    lb_b = _lower_bound(lbb_ref[...], layer)
    of_ref[0] = _hgrn_block(zqf[0], zff[0], zif[0], lb_f, st_scr.at[0], False).astype(of_ref.dtype)
    ob_ref[0] = _hgrn_block(zqb[0], zfb[0], zib[0], lb_b, st_scr.at[1], True).astype(ob_ref.dtype)

    @pl.when(n == nsteps - 1)
    def _():
        sf_ref[0] = st_scr[0]
        sb_ref[0] = st_scr[1]


def _hgrn(z, lb_raw, s0f, s0b, layer):
    bsz, t, _ = z.shape
    c = min(t, HGRN_BLOCK)
    n = t // c
    depth = lb_raw.shape[1]
    zspec = lambda colblk, rev: pl.BlockSpec(
        (1, c, WIDTH), (lambda b, i: (b, n - 1 - i, colblk)) if rev else (lambda b, i: (b, i, colblk)))
    sspec = pl.BlockSpec((1, HEADS, HEAD_DIM, HEAD_DIM), lambda b, i: (b, 0, 0, 0))
    ospec = lambda rev: pl.BlockSpec((1, c, WIDTH), (lambda b, i: (b, n - 1 - i, 0)) if rev else (lambda b, i: (b, i, 0)))
    st_shape = jax.ShapeDtypeStruct((bsz, HEADS, HEAD_DIM, HEAD_DIM), F32)
    return pl.pallas_call(
        functools.partial(_hgrn_kernel, layer=layer, nsteps=n),
        out_shape=(jax.ShapeDtypeStruct((bsz, t, WIDTH), BF16), jax.ShapeDtypeStruct((bsz, t, WIDTH), BF16),
                   st_shape, st_shape),
        grid=(bsz, n),
        in_specs=[
            zspec(COL_Q // WIDTH, False), zspec(COL_FF // WIDTH, False), zspec(COL_I // WIDTH, False),
            zspec(COL_Q // WIDTH, True), zspec(COL_FB // WIDTH, True), zspec(COL_I // WIDTH, True),
            pl.BlockSpec((depth, WIDTH), lambda b, i: (0, 0)),
            pl.BlockSpec((depth, WIDTH), lambda b, i: (0, 0)),
            sspec, sspec,
        ],
        out_specs=(ospec(False), ospec(True), sspec, sspec),
        scratch_shapes=[pltpu.VMEM((2, HEADS, HEAD_DIM, HEAD_DIM), F32)],
        compiler_params=_cparams("arbitrary", "arbitrary"),
        name="hgrn",
    )(z, z, z, z, z, z, lb_raw[0], lb_raw[1], s0f, s0b)


def _fourier_kernel(z_ref, dft_ref, dfx_ref, cd_ref, sd_ref, lo_ref, hi_ref, r_scr, *, t, tp, scale):
    @pl.when(pl.program_id(1) == 0)
    def _():
        for g in range(HEADS):
            zg = z_ref[0, :, g * HEAD_DIM:(g + 1) * HEAD_DIM]
            r_scr[0:t, g * HEAD_DIM:(g + 1) * HEAD_DIM] = jnp.dot(
                zg, cd_ref[...], preferred_element_type=F32).astype(BF16)
            r_scr[t:2 * t, g * HEAD_DIM:(g + 1) * HEAD_DIM] = (-jnp.dot(
                zg, sd_ref[...], preferred_element_type=F32)).astype(BF16)

    ca = jnp.dot(dft_ref[:, 0:t], r_scr[0:t, :], preferred_element_type=F32)
    sb = jnp.dot(dft_ref[:, t:2 * t], r_scr[t:2 * t, :], preferred_element_type=F32)
    lo_ref[0] = ((ca + sb) * scale).astype(lo_ref.dtype)
    xa = jnp.dot(dfx_ref[:, 0:t], r_scr[0:t, :], preferred_element_type=F32)
    xs = jnp.dot(dfx_ref[:, t:2 * t], r_scr[t:2 * t, :], preferred_element_type=F32)
    src = jnp.concatenate([((ca - sb) * scale).astype(BF16), ((xa - xs) * scale).astype(BF16)], axis=0)
    rr = lax.broadcasted_iota(jnp.int32, (tp, tp + DFX_ROWS), 0)
    cc = lax.broadcasted_iota(jnp.int32, (tp, tp + DFX_ROWS), 1)
    flip = jnp.where(cc == tp - rr, 1.0, 0.0).astype(BF16)
    hi_ref[0] = jnp.dot(flip, src, preferred_element_type=F32).astype(hi_ref.dtype)


def _fourier(z, dft, dfx, cd, sd):
    bsz, t, _ = z.shape
    half = t // 2
    tp = min(half, 512)
    nt = half // tp
    lo, hi = pl.pallas_call(
        functools.partial(_fourier_kernel, t=t, tp=tp, scale=1.0 / math.sqrt(t * HEAD_DIM)),
        out_shape=(jax.ShapeDtypeStruct((bsz, half, WIDTH), BF16), jax.ShapeDtypeStruct((bsz, half, WIDTH), BF16)),
        grid=(bsz, nt),
        in_specs=[
            pl.BlockSpec((1, t, WIDTH), lambda b, i: (b, 0, COL_FOURIER // WIDTH)),
            pl.BlockSpec((tp, 2 * t), lambda b, i: (i, 0)),
            pl.BlockSpec((DFX_ROWS, 2 * t), lambda b, i: (i, 0)),
            pl.BlockSpec((HEAD_DIM, HEAD_DIM), lambda b, i: (0, 0)),
            pl.BlockSpec((HEAD_DIM, HEAD_DIM), lambda b, i: (0, 0)),
        ],
        out_specs=(pl.BlockSpec((1, tp, WIDTH), lambda b, i: (b, i, 0)),
                   pl.BlockSpec((1, tp, WIDTH), lambda b, i: (b, nt - 1 - i, 0))),
        scratch_shapes=[pltpu.VMEM((2 * t, WIDTH), BF16)],
        compiler_params=_cparams("arbitrary", "arbitrary"),
        name="fourier",
    )(z, dft, dfx, cd, sd)
    return jnp.concatenate([lo, hi], axis=1)


def _dft_tables(t):
    r = int(round(math.sqrt(t)))
    assert r * r == t
    half = t // 2
    tp = min(half, 512)
    s = np.arange(t, dtype=np.int64)
    a_ang = 2.0 * np.pi * ((np.arange(r // 2)[:, None] * r * s[None, :]) % t) / t
    b_ang = 2.0 * np.pi * ((np.arange(r)[:, None] * s[None, :]) % t) / t
    ac, asn = jnp.asarray(np.cos(a_ang), F32)[:, None, :], jnp.asarray(np.sin(a_ang), F32)[:, None, :]
    bc, bsn = jnp.asarray(np.cos(b_ang), F32)[None, :, :], jnp.asarray(np.sin(b_ang), F32)[None, :, :]
    ct = (ac * bc - asn * bsn).reshape(half, t)
    st = (asn * bc + ac * bsn).reshape(half, t)
    x_ang = 2.0 * np.pi * ((np.arange(tp, half + 1, tp)[:, None] * s[None, :]) % t) / t
    xrow = np.zeros((half // tp, DFX_ROWS, 2 * t), np.float32)
    xrow[:, 0, :t] = np.cos(x_ang)
    xrow[:, 0, t:] = np.sin(x_ang)
    dfx = jnp.asarray(xrow.reshape(-1, 2 * t), F32).astype(BF16)
    return jnp.concatenate([ct, st], axis=1).astype(BF16), dfx


def _channel_dft():
    d = np.arange(HEAD_DIM)
    ang = 2.0 * np.pi * ((d[:, None] * d[None, :]) % HEAD_DIM) / HEAD_DIM
    return jnp.asarray(np.cos(ang), F32).astype(BF16), jnp.asarray(np.sin(ang), F32).astype(BF16)


def _merge_kernel(hf_ref, hb_ref, ga_ref, fo_ref, u_ref, v_ref, g0_ref, g1_ref, g2_ref, x_ref,
                  gnw_ref, sw_ref, sb_ref, snw_ref, wb_ref, wo_ref, nw1_ref, nw2_ref,
                  gate1_ref, sh2_ref, sc2_ref, rw_ref,
                  xo_ref, h2_ref, aff_ref):
    tm = x_ref.shape[1]
    hsum = hf_ref[0].astype(F32) + hb_ref[0].astype(F32)
    ga = ga_ref[0].astype(F32)
    oa = []
    for h in range(HEADS):
        sl = slice(h * HEAD_DIM, (h + 1) * HEAD_DIM)
        oa.append(_rms(hsum[:, sl], gnw_ref[...]))
    o_a = (jnp.concatenate(oa, axis=1) * (ga * _sigmoid(ga))).astype(BF16)
    u = _gelu(u_ref[0].astype(F32))
    vv = _gelu(v_ref[0].astype(F32))
    rows = []
    for ch in range(tm // SGU_CHUNK):
        rs = slice(ch * SGU_CHUNK, (ch + 1) * SGU_CHUNK)
        cols = []
        for g in range(HEADS):
            sl = slice(g * HEAD_DIM, (g + 1) * HEAD_DIM)
            vn = _rms(vv[rs, sl], snw_ref[g:g + 1, :]).astype(BF16)
            cols.append(jnp.dot(sw_ref[g], vn, preferred_element_type=F32) + sb_ref[g])
        rows.append(jnp.concatenate(cols, axis=1))
    o_c = (u * jnp.concatenate(rows, axis=0)).astype(BF16)
    merged = _sigmoid(g0_ref[0].astype(F32)) * jnp.dot(o_a, wb_ref[0, 0], preferred_element_type=F32)
    merged += _sigmoid(g1_ref[0].astype(F32)) * jnp.dot(fo_ref[0], wb_ref[0, 1], preferred_element_type=F32)
    merged += _sigmoid(g2_ref[0].astype(F32)) * jnp.dot(o_c, wb_ref[0, 2], preferred_element_type=F32)
    y = jnp.dot(merged.astype(BF16), wo_ref[0], preferred_element_type=F32)
    xn = x_ref[0] + gate1_ref[0] * _rms(y, nw1_ref[...])
    xo_ref[0] = xn
    h2 = _rms(xn, nw2_ref[...]) * (1.0 + sc2_ref[0]) + sh2_ref[0]
    _store_token_tiles(h2_ref.at[0], h2)
    logits = lax.dot_general(rw_ref[...], h2, (((1,), (1,)), ((), ())),
                             precision=lax.Precision.HIGHEST, preferred_element_type=F32)
    ex = jnp.exp(logits - jnp.max(logits, axis=0, keepdims=True))
    aff_ref[0] = ex / jnp.sum(ex, axis=0, keepdims=True)


def _merge(z, hf, hb, fo, x, gnw, sgu_w, sgu_bias, snw, wb, wo, nw1, nw2, gate1, sh2, sc2, rw_t, layer):
    bsz, t, d = x.shape
    tm = min(t, 512)
    zw = lambda colblk: pl.BlockSpec((1, tm, WIDTH), lambda b, i: (b, i, colblk))
    zd = lambda colblk: pl.BlockSpec((1, tm, d), lambda b, i: (b, i, colblk))
    tok = lambda width: pl.BlockSpec((1, tm, width), lambda b, i: (b, i, 0))
    full = lambda *shape: pl.BlockSpec(shape, lambda b, i: (0,) * len(shape))
    mod = pl.BlockSpec((1, 1, d), lambda b, i: (b, 0, 0))
    return pl.pallas_call(
        _merge_kernel,
        out_shape=(jax.ShapeDtypeStruct((bsz, t, d), F32), jax.ShapeDtypeStruct((bsz, t * TILE_ROWS, LANES), F32),
                   jax.ShapeDtypeStruct((bsz, N_EXPERTS, t), F32)),
        grid=(bsz, t // tm),
        in_specs=[
            tok(WIDTH), tok(WIDTH), zw(COL_GA // WIDTH), tok(WIDTH), zw(COL_U // WIDTH), zw(COL_V // WIDTH),
            zd(COL_GATES // d), zd(COL_GATES // d + 1), zd(COL_GATES // d + 2), tok(d),
            full(1, HEAD_DIM), full(HEADS, SGU_CHUNK, SGU_CHUNK), full(HEADS, SGU_CHUNK, HEAD_DIM),
            full(HEADS, HEAD_DIM), pl.BlockSpec((1, N_BRANCH, WIDTH, d), lambda b, i: (layer, 0, 0, 0)),
            pl.BlockSpec((1, d, d), lambda b, i: (layer, 0, 0)), full(1, d), full(1, d),
            mod, mod, mod, full(N_EXPERTS, d),
        ],
        out_specs=(tok(d), pl.BlockSpec((1, tm * TILE_ROWS, LANES), lambda b, i: (b, i, 0)),
                   pl.BlockSpec((1, N_EXPERTS, tm), lambda b, i: (b, 0, i))),
        compiler_params=_cparams("arbitrary", "arbitrary"),
        name="merge",
    )(hf, hb, z, fo, z, z, z, z, z, x, gnw, sgu_w, sgu_bias, snw, wb, wo, nw1, nw2, gate1, sh2, sc2, rw_t)


def _prefix_count(mask, t):
    lanes = 128
    s_i = lax.broadcasted_iota(jnp.int32, (lanes, lanes), 0)
    t_i = lax.broadcasted_iota(jnp.int32, (lanes, lanes), 1)
    upper = jnp.where(s_i < t_i, 1.0, 0.0).astype(BF16)
    off = jnp.zeros((mask.shape[0], 1), F32)
    parts = []
    for blk in range(t // lanes):
        m = mask[:, blk * lanes:(blk + 1) * lanes]
        parts.append(jnp.dot(m.astype(BF16), upper, preferred_element_type=F32) + off)
        off = off + jnp.sum(m, axis=1, keepdims=True)
    return jnp.concatenate(parts, axis=1)


def _select_kernel(aff_ref, idx_ref, gate_ref, *, t, cap):
    a = aff_ref[0]
    ne = a.shape[0]

    def search(_, carry):
        lo, hi = carry
        mid = lo + ((hi - lo) >> 1)
        cnt = jnp.sum(jnp.where(a >= pltpu.bitcast(mid, F32), 1.0, 0.0), axis=1, keepdims=True)
        ok = cnt >= cap
        return jnp.where(ok, mid, lo), jnp.where(ok, hi, mid)

    lo0 = jnp.zeros((ne, 1), jnp.int32)
    hi0 = jnp.full((ne, 1), 0x7F800000, jnp.int32)
    thr, _ = lax.fori_loop(0, 31, search, (lo0, hi0))
    ge = jnp.where(a >= pltpu.bitcast(thr, F32), 1.0, 0.0)
    gt = jnp.where(a >= pltpu.bitcast(thr + 1, F32), 1.0, 0.0)
    eq = ge - gt
    need = cap - jnp.sum(gt, axis=1, keepdims=True)
    sel = gt + eq * jnp.where(_prefix_count(eq, t) < need, 1.0, 0.0)
    pos = jnp.where(sel > 0.0, _prefix_count(sel, t), -1.0)

    nrow = cap // SLOT_COLS
    prow = jnp.floor(pos * (1.0 / SLOT_COLS))
    pcol = pos - prow * SLOT_COLS
    tok = lax.broadcasted_iota(jnp.int32, (1, t), 1)
    a1 = a.astype(BF16)
    d1 = a - a1.astype(F32)
    a2 = d1.astype(BF16)
    a3 = (d1 - a2.astype(F32)).astype(BF16)
    t_hi = (tok >> 6).astype(F32)
    t_lo = (tok & 63).astype(F32)
    r_id = lax.broadcasted_iota(jnp.int32, (nrow, 1), 0).astype(F32)
    c_id = lax.broadcasted_iota(jnp.int32, (SLOT_COLS, 1), 0).astype(F32)
    for e in range(ne):
        in_row = prow[e:e + 1] == r_id
        vals = (t_hi, t_lo, a1[e:e + 1].astype(F32), a2[e:e + 1].astype(F32), a3[e:e + 1].astype(F32))
        lhs = jnp.concatenate([jnp.where(in_row, v, 0.0).astype(BF16) for v in vals], axis=0)
        rhs = jnp.where(pcol[e:e + 1] == c_id, 1.0, 0.0).astype(BF16)
        res = lax.dot_general(lhs, rhs, (((1,), (1,)), ((), ())), preferred_element_type=F32)
        idx_ref[0, e] = (res[0:nrow] * 64.0 + res[nrow:2 * nrow]).astype(jnp.int32)
        gate_ref[0, e] = res[2 * nrow:3 * nrow] + res[3 * nrow:4 * nrow] + res[4 * nrow:5 * nrow]


def _select(aff, cap):
    bsz, ne, t = aff.shape
    nrow = cap // SLOT_COLS
    ospec = pl.BlockSpec((1, ne, nrow, SLOT_COLS), lambda b: (b, 0, 0, 0))
    idx, gate = pl.pallas_call(
        functools.partial(_select_kernel, t=t, cap=cap),
        out_shape=(jax.ShapeDtypeStruct((bsz, ne, nrow, SLOT_COLS), jnp.int32),
                   jax.ShapeDtypeStruct((bsz, ne, nrow, SLOT_COLS), F32)),
        grid=(bsz,),
        in_specs=[pl.BlockSpec((1, ne, t), lambda b: (b, 0, 0))],
        out_specs=(ospec, ospec),
        compiler_params=_cparams("arbitrary"),
        name="select",
    )(aff)
    return idx.reshape(bsz, ne, cap), gate.reshape(bsz, ne, cap)


def _experts_kernel(idx_ref, idxn_ref, h_hbm, w13_ref, w2_ref, y_ref, w13s, w2s, xe, sem, fence,
                    *, ngroups, bg, t, cap, nsteps):
    e = pl.program_id(0)
    g = pl.program_id(1)
    step = e * ngroups + g
    slot = step % 2
    f = w2_ref.shape[2]
    fc = min(f, EXPERT_FCHUNK)
    nck = f // fc
    m = bg * cap
    if nck > 2:
        first = m // 8
        bounds = [0] + [first + (m - first) * i // (nck - 2) for i in range(nck - 1)]
    else:
        bounds = [0, m]

    def row_copy(src_row, dst_slot, j):
        return pltpu.make_async_copy(h_hbm.at[pl.ds(pl.multiple_of(src_row * TILE_ROWS, TILE_ROWS), TILE_ROWS), :],
                                     xe.at[dst_slot, pl.ds(j * TILE_ROWS, TILE_ROWS), :], sem.at[dst_slot])

    @pl.when(step == 0)
    def _():
        for bi in range(bg):
            def body(j, carry):
                row_copy((g * bg + bi) * t + idx_ref[0, bi, j], slot, bi * cap + j).start()
                return carry
            lax.fori_loop(0, cap, body, 0, unroll=8)

    @pl.when(g == 0)
    def _():
        w13s[...] = w13_ref[0, 0].astype(BF16)
        w2s[...] = w2_ref[0, 0].astype(BF16)

    def wait_body(j, carry):
        row_copy(0, slot, j).wait()
        return carry
    lax.fori_loop(0, m, wait_body, 0, unroll=8)

    last = step + 1 >= nsteps
    nbase = jnp.where(last, g, (step + 1) % ngroups) * (bg * t)
    xb = _load_token_tiles(xe.at[slot], m).astype(BF16)
    y = None
    for k in range(nck):
        a = jnp.dot(xb, w13s[:, k * fc:(k + 1) * fc], preferred_element_type=F32)
        gate = jnp.dot(xb, w13s[:, f + k * fc:f + (k + 1) * fc], preferred_element_type=F32)
        hid = (a * _sigmoid(a) * gate).astype(BF16)
        yk = jnp.dot(hid, w2s[k * fc:(k + 1) * fc, :], preferred_element_type=F32)
        y = yk if y is None else y + yk
        if k < len(bounds) - 1:
            for j in range(bounds[k], bounds[k + 1]):
                row_copy(nbase + (j // cap) * t + idxn_ref[0, j // cap, j % cap], 1 - slot, j).start()
            pl.semaphore_signal(fence, 1)
            pl.semaphore_wait(fence, 1)
    for bi in range(bg):
        _store_token_tiles(y_ref.at[bi, 0], y[bi * cap:(bi + 1) * cap])

    @pl.when(last)
    def _():
        def drain(j, carry):
            row_copy(0, 1 - slot, j).wait()
            return carry
        lax.fori_loop(0, m, drain, 0, unroll=8)


def _experts(h2, idx, w13, w2, layer):
    bsz = h2.shape[0]
    t = h2.shape[1] // TILE_ROWS
    d = D_MODEL
    ne, cap = idx.shape[1], idx.shape[2]
    f = w2.shape[2]
    bg = max(1, min(bsz, EXPERT_MIN_ROWS // cap))
    ngroups = bsz // bg
    nsteps = ne * ngroups
    idx3 = idx.transpose(1, 0, 2).reshape(nsteps, bg, cap)

    def nxt(e, g):
        return (jnp.minimum(e * ngroups + g + 1, nsteps - 1), 0, 0)

    return pl.pallas_call(
        functools.partial(_experts_kernel, ngroups=ngroups, bg=bg, t=t, cap=cap, nsteps=nsteps),
        out_shape=jax.ShapeDtypeStruct((bsz, ne, cap * TILE_ROWS, LANES), F32),
        grid=(ne, ngroups),
        in_specs=[
            pl.BlockSpec((1, bg, cap), lambda e, g: (e * ngroups + g, 0, 0), memory_space=pltpu.SMEM),
            pl.BlockSpec((1, bg, cap), nxt, memory_space=pltpu.SMEM),
            pl.BlockSpec(memory_space=pl.ANY),
            pl.BlockSpec((1, 1, d, 2 * f), lambda e, g: (layer, e, 0, 0)),
            pl.BlockSpec((1, 1, f, d), lambda e, g: (layer, e, 0, 0)),
        ],
        out_specs=pl.BlockSpec((bg, 1, cap * TILE_ROWS, LANES), lambda e, g: (g, e, 0, 0)),
        scratch_shapes=[pltpu.VMEM((d, 2 * f), BF16), pltpu.VMEM((f, d), BF16),
                        pltpu.VMEM((2, bg * cap * TILE_ROWS, LANES), F32), pltpu.SemaphoreType.DMA((2,)),
                        pltpu.SemaphoreType.REGULAR],
        compiler_params=_cparams("arbitrary", "arbitrary"),
        name="experts",
    )(idx3, idx3, h2.reshape(bsz * t * TILE_ROWS, LANES), w13, w2)


def _combine_kernel(idx_ref, gate_ref, y_ref, m_ref, *, cap):
    @pl.when(pl.program_id(1) == 0)
    def _():
        m_ref[...] = jnp.zeros_like(m_ref)

    group = 8

    def body(jj, carry):
        base = jj * group
        dst = [pl.ds(pl.multiple_of(idx_ref[0, 0, base + r] * TILE_ROWS, TILE_ROWS), TILE_ROWS)
               for r in range(group)]
        acc = [m_ref[0, d, :] for d in dst]
        for r in range(group):
            src = pl.ds(pl.multiple_of((base + r) * TILE_ROWS, TILE_ROWS), TILE_ROWS)
            m_ref[0, dst[r], :] = acc[r] + gate_ref[0, 0, base + r] * y_ref[0, 0, src, :]
        return carry
    lax.fori_loop(0, cap // group, body, 0, unroll=2)


def _combine(y, idx, gate, t):
    bsz, ne, cap = idx.shape
    idx3 = idx.reshape(bsz * ne, 1, cap)
    gate3 = gate.reshape(bsz * ne, 1, cap)
    sspec = pl.BlockSpec((1, 1, cap), lambda b, e: (b * ne + e, 0, 0), memory_space=pltpu.SMEM)
    return pl.pallas_call(
        functools.partial(_combine_kernel, cap=cap),
        out_shape=jax.ShapeDtypeStruct((bsz, t * TILE_ROWS, LANES), F32),
        grid=(bsz, ne),
        in_specs=[sspec, sspec, pl.BlockSpec((1, 1, cap * TILE_ROWS, LANES), lambda b, e: (b, e, 0, 0))],
        out_specs=pl.BlockSpec((1, t * TILE_ROWS, LANES), lambda b, e: (b, 0, 0)),
        compiler_params=_cparams("arbitrary", "arbitrary"),
        name="combine",
    )(idx3, gate3, y)


def _residual_kernel(x_ref, m_ref, g_ref, nw_ref, o_ref):
    m = _load_token_tiles(m_ref.at[0], x_ref.shape[1])
    o_ref[0] = x_ref[0] + g_ref[0] * _rms(m, nw_ref[...])


def _residual(x, m, gate, nw):
    bsz, t, d = x.shape
    tm = min(t, 512)
    tok = pl.BlockSpec((1, tm, d), lambda b, i: (b, i, 0))
    return pl.pallas_call(
        _residual_kernel,
        out_shape=jax.ShapeDtypeStruct((bsz, t, d), F32),
        grid=(bsz, t // tm),
        in_specs=[tok, pl.BlockSpec((1, tm * TILE_ROWS, LANES), lambda b, i: (b, i, 0)),
                  pl.BlockSpec((1, 1, d), lambda b, i: (b, 0, 0)), pl.BlockSpec((1, d), lambda b, i: (0, 0))],
        out_specs=tok,
        compiler_params=_cparams("arbitrary", "arbitrary"),
        name="residual",
    )(x, m, gate, nw.reshape(1, d))


def _token_mix_and_moe(stream, z, hf, hb, dft, cdsd, mods, lw):
    sh2, sc2, g1, g2 = mods
    t = stream.shape[1]
    fo = _fourier(z, *dft, *cdsd)
    mid, h2, aff = _merge(z, hf, hb, fo, stream, lw["gnw"], lw["sgu_w"], lw["sgu_bias"], lw["snw"], lw["wb"],
                          lw["wo"], lw["nw1"], lw["nw2"], g1, sh2, sc2, lw["rw_t"], lw["layer"])
    cap = CAPACITY_FACTOR * t // N_EXPERTS
    idx, gate = _select(aff, cap)
    y = _experts(h2, idx, lw["w13"], lw["w2"], lw["layer"])
    m = _combine(y, idx, gate, t)
    return _residual(mid, m, g2, lw["nw3"])


def kernel(x, c, ctx, c_ctx, ada_w, ada_b, norm_w, w_in, hgrn_lb_raw, hgrn_gnorm_w, sgu_w, sgu_b, sgu_norm_w,
           w_branch, w_out, router_w, exp_w13, exp_w2):
    bsz, t, d = x.shape
    tc = ctx.shape[1]
    depth = ada_w.shape[0]
    assert d == D_MODEL and bsz + 1 <= MOD_ROWS and w_in.shape[2] == IN_COLS

    cvec = jnp.zeros((MOD_ROWS, d), F32).at[:bsz].set(c).at[bsz].set(c_ctx)
    mod = _adaln(cvec, ada_w, ada_b).reshape(depth, MOD_ROWS, 6, d)
    dft_lat, dft_ctx = _dft_tables(t), _dft_tables(tc)
    cdsd = _channel_dft()
    zero_state = jnp.zeros((bsz, HEADS, HEAD_DIM, HEAD_DIM), F32)
    w_in_b, w_branch_b, w_out_b = w_in.astype(BF16), w_branch.astype(BF16), w_out.astype(BF16)

    for l in range(depth):
        last = l == depth - 1
        lat = [mod[l, :bsz, k][:, None, :] for k in range(6)]
        cm = [jnp.broadcast_to(mod[l, bsz, k][None, None, :], (bsz, 1, d)) for k in range(6)]
        lw = dict(
            gnw=hgrn_gnorm_w[l].reshape(1, HEAD_DIM),
            sgu_w=sgu_w[l].astype(BF16),
            sgu_bias=jnp.broadcast_to(sgu_b[l][:, :, None], (HEADS, SGU_CHUNK, HEAD_DIM)),
            snw=sgu_norm_w[l],
            wb=w_branch_b, wo=w_out_b, layer=l,
            nw1=norm_w[l, 1].reshape(1, d), nw2=norm_w[l, 2].reshape(1, d), nw3=norm_w[l, 3],
            rw_t=router_w[l].T,
            w13=exp_w13, w2=exp_w2,
        )
        lb_raw = hgrn_lb_raw
        z_ctx = _inproj(ctx, norm_w[l, 0], cm[0], cm[1], w_in_b, l, A_IN if last else IN_COLS)
        hf_c, hb_c, s_f, s_b = _hgrn(z_ctx, lb_raw, zero_state, zero_state, l)
        z_lat = _inproj(x, norm_w[l, 0], lat[0], lat[1], w_in_b, l, IN_COLS)
        hf, hb, _, _ = _hgrn(z_lat, lb_raw, s_f, s_b, l)
        x = _token_mix_and_moe(x, z_lat, hf, hb, dft_lat, cdsd, (lat[3], lat[4], lat[2], lat[5]), lw)
        if not last:
            ctx = _token_mix_and_moe(ctx, z_ctx, hf_c, hb_c, dft_ctx, cdsd, (cm[3], cm[4], cm[2], cm[5]), lw)
    return x
```

```python
import functools
import math

import numpy as np
import jax
import jax.numpy as jnp
from jax import lax
from jax.experimental import pallas as pl
from jax.experimental.pallas import tpu as pltpu

F32 = jnp.float32
BF16 = jnp.bfloat16

D_MODEL = 1024
HEADS = 4
HEAD_DIM = 128
WIDTH = HEADS * HEAD_DIM
SGU_CHUNK = 128
N_BRANCH = 3
N_EXPERTS = 16
CAPACITY_FACTOR = 2
EPS = 1e-6
LB_FLOOR = 1e-20

COL_Q, COL_FF, COL_FB, COL_I = 0, WIDTH, 2 * WIDTH, 3 * WIDTH
COL_GA = 4 * WIDTH
COL_FOURIER = 5 * WIDTH
COL_U = 6 * WIDTH
COL_V = 7 * WIDTH
COL_GATES = 8 * WIDTH
IN_COLS = COL_GATES + N_BRANCH * D_MODEL
A_IN = 4 * WIDTH

HGRN_CHUNK = 64
HGRN_SUB = 16
HGRN_BLOCK = 256
EXP_CLAMP = 80.0
LOG2E = 1.4426950408889634
EXP2_CLAMP = 115.0

VMEM_LIMIT = 56 * 1024 * 1024
EXPERT_FCHUNK = 256
COMBINE_MIN_ROWS = 256
EXPERT_MIN_ROWS = 1024
SLOT_COLS = 32
DFX_ROWS = 16
TILE_ROWS, LANES = 8, 128
MOD_ROWS = 16


def _cparams(*sem):
    return pltpu.CompilerParams(dimension_semantics=sem, vmem_limit_bytes=VMEM_LIMIT)


def _sigmoid(v):
    return 1.0 / (1.0 + jnp.exp2(v * (-LOG2E)))


def _rms(v, w):
    return v * lax.rsqrt(jnp.mean(v * v, axis=-1, keepdims=True) + EPS) * w


def _gelu(v):
    return 0.5 * v * (1.0 + lax.erf(v * (1.0 / math.sqrt(2.0))))


def _store_token_tiles(ref, val):
    rows = val.shape[0]
    for k in range(TILE_ROWS):
        ref[pl.ds(k, rows, stride=TILE_ROWS), :] = val[:, k * LANES:(k + 1) * LANES]


def _load_token_tiles(ref, rows):
    return jnp.concatenate([ref[pl.ds(k, rows, stride=TILE_ROWS), :] for k in range(TILE_ROWS)], axis=1)


def _adaln_kernel(c_ref, w_ref, b_ref, o_ref):
    c = c_ref[...]
    s = c * _sigmoid(c)
    o_ref[0] = jnp.dot(s, w_ref[0], precision=lax.Precision.HIGHEST, preferred_element_type=F32) + b_ref[0]


def _adaln(cvec, ada_w, ada_b):
    depth, d, n = ada_w.shape
    tn = 512
    return pl.pallas_call(
        _adaln_kernel,
        out_shape=jax.ShapeDtypeStruct((depth, MOD_ROWS, n), F32),
        grid=(depth, n // tn),
        in_specs=[
            pl.BlockSpec((MOD_ROWS, d), lambda l, j: (0, 0)),
            pl.BlockSpec((1, d, tn), lambda l, j: (l, 0, j)),
            pl.BlockSpec((1, 1, tn), lambda l, j: (l, 0, j)),
        ],
        out_specs=pl.BlockSpec((1, MOD_ROWS, tn), lambda l, j: (l, 0, j)),
        compiler_params=_cparams("arbitrary", "arbitrary"),
        name="adaln",
    )(cvec, ada_w, ada_b.reshape(depth, 1, n))


def _inproj_kernel(x_ref, nw_ref, sh_ref, sc_ref, w_ref, z_ref, *, tn):
    h = (_rms(x_ref[0], nw_ref[...]) * (1.0 + sc_ref[0]) + sh_ref[0]).astype(BF16)
    for j in range(w_ref.shape[2] // tn):
        z_ref[0, :, j * tn:(j + 1) * tn] = jnp.dot(
            h, w_ref[0, :, j * tn:(j + 1) * tn], preferred_element_type=F32).astype(z_ref.dtype)


def _inproj(x, nw, sh, sc, w, layer, n):
    bsz, t, d = x.shape
    tm = min(t, 512)
    return pl.pallas_call(
        functools.partial(_inproj_kernel, tn=1024),
        out_shape=jax.ShapeDtypeStruct((bsz, t, n), BF16),
        grid=(bsz, t // tm),
        in_specs=[
            pl.BlockSpec((1, tm, d), lambda b, i: (b, i, 0)),
            pl.BlockSpec((1, d), lambda b, i: (0, 0)),
            pl.BlockSpec((1, 1, d), lambda b, i: (b, 0, 0)),
            pl.BlockSpec((1, 1, d), lambda b, i: (b, 0, 0)),
            pl.BlockSpec((1, d, n), lambda b, i: (layer, 0, 0), pipeline_mode=pl.Buffered(1)),
        ],
        out_specs=pl.BlockSpec((1, tm, n), lambda b, i: (b, i, 0)),
        compiler_params=_cparams("arbitrary", "arbitrary"),
        name="inproj",
    )(x, nw.reshape(1, d), sh, sc, w)


def _lower_bound(raw, layer):
    m = jnp.max(raw, axis=0, keepdims=True)
    e = jnp.exp(raw - m)
    p = e / jnp.sum(e, axis=0, keepdims=True)
    lb = jnp.zeros_like(p[0:1])
    for j in range(1, layer + 1):
        lb = lb + p[j:j + 1]
    return lb


def _hgrn_block(zq, zf, zi, lb, st_ref, reverse):
    tb = zq.shape[0]
    c, sc = HGRN_CHUNK, HGRN_SUB
    ns = c // sc
    zq = zq.astype(F32)
    zf = zf.astype(F32)
    q = zq * _sigmoid(zq)
    e = jnp.exp2(jnp.maximum(zf, -EXP_CLAMP) * (-LOG2E))
    r = 1.0 / (1.0 + e)
    lbm = jnp.maximum(lb, LB_FLOOR)
    f = lbm + (1.0 - lb) * r
    kk = (1.0 - lb) * (e * r) - (lbm - lb)
    lf = jnp.log2(f)
    v = zi.astype(BF16)

    row = lax.broadcasted_iota(jnp.int32, (tb, tb), 0)
    col = lax.broadcasted_iota(jnp.int32, (tb, tb), 1)
    absorbed = (col >= row) if reverse else (col <= row)
    trib = jnp.where(absorbed & ((row // c) == (col // c)), 1.0, 0.0).astype(BF16)
    l1 = lf.astype(BF16)
    l2 = (lf - l1.astype(F32)).astype(BF16)
    b = jnp.dot(trib, l1, preferred_element_type=F32) + jnp.dot(trib, l2, preferred_element_type=F32)
    b_excl = b - lf

    refs = []
    for s in range(tb // sc):
        first = s * sc + (sc - 1 if reverse else 0)
        refs.append(b_excl[first:first + 1])
    r_own = jnp.concatenate([jnp.broadcast_to(x, (sc, WIDTH)) for x in refs], axis=0)
    q_own = q * jnp.exp2(b - r_own)
    k_own = (kk * jnp.exp2(jnp.minimum(r_own - b, EXP2_CLAMP))).astype(BF16)
    zero_tile = jnp.zeros((sc, HEAD_DIM), BF16)
    ti = lax.broadcasted_iota(jnp.int32, (c, c), 0)
    si = lax.broadcasted_iota(jnp.int32, (c, c), 1)
    tri_c = (si >= ti) if reverse else (si <= ti)

    nch = tb // c
    order = list(range(nch - 1, -1, -1)) if reverse else list(range(nch))
    qe_c, g_c, sc_c, v_c, kd_c = {}, {}, {}, {}, {}
    for ch in order:
        lo = ch * c
        b_c = b[lo:lo + c]
        b_end = b_c[0:1] if reverse else b_c[c - 1:c]
        qe_c[ch] = (q[lo:lo + c] * jnp.exp2(b_c)).astype(BF16)
        kd_c[ch] = (kk[lo:lo + c] * jnp.exp2(b_end - b_c)).astype(BF16)
        g_c[ch] = jnp.exp2(b_end)
        v_c[ch] = v[lo:lo + c]
        qij = {}
        for i in range(ns):
            for j in range(ns):
                if (j >= i) if reverse else (j <= i):
                    gi = jnp.exp2(refs[ch * ns + i] - refs[ch * ns + j])
                    qij[i, j] = (q_own[lo + i * sc:lo + (i + 1) * sc] * gi).astype(BF16)
        for h in range(HEADS):
            sl = slice(h * HEAD_DIM, (h + 1) * HEAD_DIM)
            q_aug = jnp.concatenate([
                jnp.concatenate([qij[i, j][:, sl] if (i, j) in qij else zero_tile for j in range(ns)], axis=1)
                for i in range(ns)], axis=0)
            k_aug = jnp.concatenate([
                jnp.concatenate([k_own[lo + i * sc:lo + (i + 1) * sc, sl] if j == i else zero_tile
                                 for j in range(ns)], axis=1)
                for i in range(ns)], axis=0)
            scores = lax.dot_general(q_aug, k_aug, (((1,), (1,)), ((), ())), preferred_element_type=F32)
            sc_c[ch, h] = jnp.where(tri_c, scores, 0.0).astype(BF16)
    o_c, upd_c = {}, {}
    for ch in order:
        for h in range(HEADS):
            sl = slice(h * HEAD_DIM, (h + 1) * HEAD_DIM)
            o_c[ch, h] = jnp.dot(sc_c[ch, h], v_c[ch][:, sl], preferred_element_type=F32)
            upd_c[ch, h] = lax.dot_general(v_c[ch][:, sl], kd_c[ch][:, sl], (((0,), (0,)), ((), ())),
                                           preferred_element_type=F32)
    for h in range(HEADS):
        sl = slice(h * HEAD_DIM, (h + 1) * HEAD_DIM)
        st = st_ref[h]
        for ch in order:
            o_c[ch, h] = o_c[ch, h] + lax.dot_general(qe_c[ch][:, sl], st.astype(BF16), (((1,), (1,)), ((), ())),
                                                      preferred_element_type=F32)
            st = st * g_c[ch][:, sl] + upd_c[ch, h]
        st_ref[h] = st
    return jnp.concatenate([jnp.concatenate([o_c[ch, h] for h in range(HEADS)], axis=1) for ch in range(nch)], axis=0)


def _hgrn_kernel(zqf, zff, zif, zqb, zfb, zib, lbf_ref, lbb_ref, s0f, s0b,
                 of_ref, ob_ref, sf_ref, sb_ref, st_scr, *, layer, nsteps):
    n = pl.program_id(1)

    @pl.when(n == 0)
    def _():
        st_scr[0] = s0f[0]
        st_scr[1] = s0b[0]

    lb_f = _lower_bound(lbf_ref[...], layer)
    lb_b = _lower_bound(lbb_ref[...], layer)
    of_ref[0] = _hgrn_block(zqf[0], zff[0], zif[0], lb_f, st_scr.at[0], False).astype(of_ref.dtype)
    ob_ref[0] = _hgrn_block(zqb[0], zfb[0], zib[0], lb_b, st_scr.at[1], True).astype(ob_ref.dtype)

    @pl.when(n == nsteps - 1)
    def _():
        sf_ref[0] = st_scr[0]
        sb_ref[0] = st_scr[1]


def _hgrn(z, lb_raw, s0f, s0b, layer):
    bsz, t, _ = z.shape
    c = min(t, HGRN_BLOCK)
    n = t // c
    depth = lb_raw.shape[1]
    zspec = lambda colblk, rev: pl.BlockSpec(
        (1, c, WIDTH), (lambda b, i: (b, n - 1 - i, colblk)) if rev else (lambda b, i: (b, i, colblk)))
    sspec = pl.BlockSpec((1, HEADS, HEAD_DIM, HEAD_DIM), lambda b, i: (b, 0, 0, 0))
    ospec = lambda rev: pl.BlockSpec((1, c, WIDTH), (lambda b, i: (b, n - 1 - i, 0)) if rev else (lambda b, i: (b, i, 0)))
    st_shape = jax.ShapeDtypeStruct((bsz, HEADS, HEAD_DIM, HEAD_DIM), F32)
    return pl.pallas_call(
        functools.partial(_hgrn_kernel, layer=layer, nsteps=n),
        out_shape=(jax.ShapeDtypeStruct((bsz, t, WIDTH), BF16), jax.ShapeDtypeStruct((bsz, t, WIDTH), BF16),
                   st_shape, st_shape),
        grid=(bsz, n),
        in_specs=[
            zspec(COL_Q // WIDTH, False), zspec(COL_FF // WIDTH, False), zspec(COL_I // WIDTH, False),
            zspec(COL_Q // WIDTH, True), zspec(COL_FB // WIDTH, True), zspec(COL_I // WIDTH, True),
            pl.BlockSpec((depth, WIDTH), lambda b, i: (0, 0)),
            pl.BlockSpec((depth, WIDTH), lambda b, i: (0, 0)),
            sspec, sspec,
        ],
        out_specs=(ospec(False), ospec(True), sspec, sspec),
        scratch_shapes=[pltpu.VMEM((2, HEADS, HEAD_DIM, HEAD_DIM), F32)],
        compiler_params=_cparams("arbitrary", "arbitrary"),
        name="hgrn",
    )(z, z, z, z, z, z, lb_raw[0], lb_raw[1], s0f, s0b)


def _fourier_kernel(z_ref, dft_ref, dfx_ref, cd_ref, sd_ref, lo_ref, hi_ref, r_scr, *, t, tp, scale):
    @pl.when(pl.program_id(1) == 0)
    def _():
        for g in range(HEADS):
            zg = z_ref[0, :, g * HEAD_DIM:(g + 1) * HEAD_DIM]
            r_scr[0:t, g * HEAD_DIM:(g + 1) * HEAD_DIM] = jnp.dot(
                zg, cd_ref[...], preferred_element_type=F32).astype(BF16)
            r_scr[t:2 * t, g * HEAD_DIM:(g + 1) * HEAD_DIM] = (-jnp.dot(
                zg, sd_ref[...], preferred_element_type=F32)).astype(BF16)

    ca = jnp.dot(dft_ref[:, 0:t], r_scr[0:t, :], preferred_element_type=F32)
    sb = jnp.dot(dft_ref[:, t:2 * t], r_scr[t:2 * t, :], preferred_element_type=F32)
    lo_ref[0] = ((ca + sb) * scale).astype(lo_ref.dtype)
    xa = jnp.dot(dfx_ref[:, 0:t], r_scr[0:t, :], preferred_element_type=F32)
    xs = jnp.dot(dfx_ref[:, t:2 * t], r_scr[t:2 * t, :], preferred_element_type=F32)
    src = jnp.concatenate([((ca - sb) * scale).astype(BF16), ((xa - xs) * scale).astype(BF16)], axis=0)
    rr = lax.broadcasted_iota(jnp.int32, (tp, tp + DFX_ROWS), 0)
    cc = lax.broadcasted_iota(jnp.int32, (tp, tp + DFX_ROWS), 1)
    flip = jnp.where(cc == tp - rr, 1.0, 0.0).astype(BF16)
    hi_ref[0] = jnp.dot(flip, src, preferred_element_type=F32).astype(hi_ref.dtype)


def _fourier(z, dft, dfx, cd, sd):
    bsz, t, _ = z.shape
    half = t // 2
    tp = min(half, 512)
    nt = half // tp
    lo, hi = pl.pallas_call(
        functools.partial(_fourier_kernel, t=t, tp=tp, scale=1.0 / math.sqrt(t * HEAD_DIM)),
        out_shape=(jax.ShapeDtypeStruct((bsz, half, WIDTH), BF16), jax.ShapeDtypeStruct((bsz, half, WIDTH), BF16)),
        grid=(bsz, nt),
        in_specs=[
            pl.BlockSpec((1, t, WIDTH), lambda b, i: (b, 0, COL_FOURIER // WIDTH)),
            pl.BlockSpec((tp, 2 * t), lambda b, i: (i, 0)),
            pl.BlockSpec((DFX_ROWS, 2 * t), lambda b, i: (i, 0)),
            pl.BlockSpec((HEAD_DIM, HEAD_DIM), lambda b, i: (0, 0)),
            pl.BlockSpec((HEAD_DIM, HEAD_DIM), lambda b, i: (0, 0)),
        ],
        out_specs=(pl.BlockSpec((1, tp, WIDTH), lambda b, i: (b, i, 0)),
                   pl.BlockSpec((1, tp, WIDTH), lambda b, i: (b, nt - 1 - i, 0))),
        scratch_shapes=[pltpu.VMEM((2 * t, WIDTH), BF16)],
        compiler_params=_cparams("arbitrary", "arbitrary"),
        name="fourier",
    )(z, dft, dfx, cd, sd)
    return jnp.concatenate([lo, hi], axis=1)


def _dft_tables(t):
    r = int(round(math.sqrt(t)))
    assert r * r == t
    half = t // 2
    tp = min(half, 512)
    s = np.arange(t, dtype=np.int64)
    a_ang = 2.0 * np.pi * ((np.arange(r // 2)[:, None] * r * s[None, :]) % t) / t
    b_ang = 2.0 * np.pi * ((np.arange(r)[:, None] * s[None, :]) % t) / t
    ac, asn = jnp.asarray(np.cos(a_ang), F32)[:, None, :], jnp.asarray(np.sin(a_ang), F32)[:, None, :]
    bc, bsn = jnp.asarray(np.cos(b_ang), F32)[None, :, :], jnp.asarray(np.sin(b_ang), F32)[None, :, :]
    ct = (ac * bc - asn * bsn).reshape(half, t)
    st = (asn * bc + ac * bsn).reshape(half, t)
    x_ang = 2.0 * np.pi * ((np.arange(tp, half + 1, tp)[:, None] * s[None, :]) % t) / t
    xrow = np.zeros((half // tp, DFX_ROWS, 2 * t), np.float32)
    xrow[:, 0, :t] = np.cos(x_ang)
    xrow[:, 0, t:] = np.sin(x_ang)
    dfx = jnp.asarray(xrow.reshape(-1, 2 * t), F32).astype(BF16)
    return jnp.concatenate([ct, st], axis=1).astype(BF16), dfx


def _channel_dft():
    d = np.arange(HEAD_DIM)
    ang = 2.0 * np.pi * ((d[:, None] * d[None, :]) % HEAD_DIM) / HEAD_DIM
    return jnp.asarray(np.cos(ang), F32).astype(BF16), jnp.asarray(np.sin(ang), F32).astype(BF16)


def _merge_kernel(hf_ref, hb_ref, ga_ref, fo_ref, u_ref, v_ref, g0_ref, g1_ref, g2_ref, x_ref,
                  gnw_ref, sw_ref, sb_ref, snw_ref, wb_ref, wo_ref, nw1_ref, nw2_ref,
                  gate1_ref, sh2_ref, sc2_ref, rw_ref,
                  xo_ref, h2_ref, aff_ref):
    tm = x_ref.shape[1]
    hsum = hf_ref[0].astype(F32) + hb_ref[0].astype(F32)
    ga = ga_ref[0].astype(F32)
    oa = []
    for h in range(HEADS):
        sl = slice(h * HEAD_DIM, (h + 1) * HEAD_DIM)
        oa.append(_rms(hsum[:, sl], gnw_ref[...]))
    o_a = (jnp.concatenate(oa, axis=1) * (ga * _sigmoid(ga))).astype(BF16)
    u = _gelu(u_ref[0].astype(F32))
    vv = _gelu(v_ref[0].astype(F32))
    rows = []
    for ch in range(tm // SGU_CHUNK):
        rs = slice(ch * SGU_CHUNK, (ch + 1) * SGU_CHUNK)
        cols = []
        for g in range(HEADS):
            sl = slice(g * HEAD_DIM, (g + 1) * HEAD_DIM)
            vn = _rms(vv[rs, sl], snw_ref[g:g + 1, :]).astype(BF16)
            cols.append(jnp.dot(sw_ref[g], vn, preferred_element_type=F32) + sb_ref[g])
        rows.append(jnp.concatenate(cols, axis=1))
    o_c = (u * jnp.concatenate(rows, axis=0)).astype(BF16)
    merged = _sigmoid(g0_ref[0].astype(F32)) * jnp.dot(o_a, wb_ref[0, 0], preferred_element_type=F32)
    merged += _sigmoid(g1_ref[0].astype(F32)) * jnp.dot(fo_ref[0], wb_ref[0, 1], preferred_element_type=F32)
    merged += _sigmoid(g2_ref[0].astype(F32)) * jnp.dot(o_c, wb_ref[0, 2], preferred_element_type=F32)
    y = jnp.dot(merged.astype(BF16), wo_ref[0], preferred_element_type=F32)
    xn = x_ref[0] + gate1_ref[0] * _rms(y, nw1_ref[...])
    xo_ref[0] = xn
    h2 = _rms(xn, nw2_ref[...]) * (1.0 + sc2_ref[0]) + sh2_ref[0]
    _store_token_tiles(h2_ref.at[0], h2)
    logits = lax.dot_general(rw_ref[...], h2, (((1,), (1,)), ((), ())),
                             precision=lax.Precision.HIGHEST, preferred_element_type=F32)
    ex = jnp.exp(logits - jnp.max(logits, axis=0, keepdims=True))
    aff_ref[0] = ex / jnp.sum(ex, axis=0, keepdims=True)


def _merge(z, hf, hb, fo, x, gnw, sgu_w, sgu_bias, snw, wb, wo, nw1, nw2, gate1, sh2, sc2, rw_t, layer):
    bsz, t, d = x.shape
    tm = min(t, 512)
    zw = lambda colblk: pl.BlockSpec((1, tm, WIDTH), lambda b, i: (b, i, colblk))
    zd = lambda colblk: pl.BlockSpec((1, tm, d), lambda b, i: (b, i, colblk))
    tok = lambda width: pl.BlockSpec((1, tm, width), lambda b, i: (b, i, 0))
    full = lambda *shape: pl.BlockSpec(shape, lambda b, i: (0,) * len(shape))
    mod = pl.BlockSpec((1, 1, d), lambda b, i: (b, 0, 0))
    return pl.pallas_call(
        _merge_kernel,
        out_shape=(jax.ShapeDtypeStruct((bsz, t, d), F32), jax.ShapeDtypeStruct((bsz, t * TILE_ROWS, LANES), F32),
                   jax.ShapeDtypeStruct((bsz, N_EXPERTS, t), F32)),
        grid=(bsz, t // tm),
        in_specs=[
            tok(WIDTH), tok(WIDTH), zw(COL_GA // WIDTH), tok(WIDTH), zw(COL_U // WIDTH), zw(COL_V // WIDTH),
            zd(COL_GATES // d), zd(COL_GATES // d + 1), zd(COL_GATES // d + 2), tok(d),
            full(1, HEAD_DIM), full(HEADS, SGU_CHUNK, SGU_CHUNK), full(HEADS, SGU_CHUNK, HEAD_DIM),
            full(HEADS, HEAD_DIM), pl.BlockSpec((1, N_BRANCH, WIDTH, d), lambda b, i: (layer, 0, 0, 0)),
            pl.BlockSpec((1, d, d), lambda b, i: (layer, 0, 0)), full(1, d), full(1, d),
            mod, mod, mod, full(N_EXPERTS, d),
        ],
        out_specs=(tok(d), pl.BlockSpec((1, tm * TILE_ROWS, LANES), lambda b, i: (b, i, 0)),
                   pl.BlockSpec((1, N_EXPERTS, tm), lambda b, i: (b, 0, i))),
        compiler_params=_cparams("arbitrary", "arbitrary"),
        name="merge",
    )(hf, hb, z, fo, z, z, z, z, z, x, gnw, sgu_w, sgu_bias, snw, wb, wo, nw1, nw2, gate1, sh2, sc2, rw_t)


def _prefix_count(mask, t):
    lanes = 128
    s_i = lax.broadcasted_iota(jnp.int32, (lanes, lanes), 0)
    t_i = lax.broadcasted_iota(jnp.int32, (lanes, lanes), 1)
    upper = jnp.where(s_i < t_i, 1.0, 0.0).astype(BF16)
    off = jnp.zeros((mask.shape[0], 1), F32)
    parts = []
    for blk in range(t // lanes):
        m = mask[:, blk * lanes:(blk + 1) * lanes]
        parts.append(jnp.dot(m.astype(BF16), upper, preferred_element_type=F32) + off)
        off = off + jnp.sum(m, axis=1, keepdims=True)
    return jnp.concatenate(parts, axis=1)


def _select_kernel(aff_ref, idx_ref, gate_ref, *, t, cap):
    a = aff_ref[0]
    ne = a.shape[0]

    def search(_, carry):
        lo, hi = carry
        mid = lo + ((hi - lo) >> 1)
        cnt = jnp.sum(jnp.where(a >= pltpu.bitcast(mid, F32), 1.0, 0.0), axis=1, keepdims=True)
        ok = cnt >= cap
        return jnp.where(ok, mid, lo), jnp.where(ok, hi, mid)

    lo0 = jnp.zeros((ne, 1), jnp.int32)
    hi0 = jnp.full((ne, 1), 0x7F800000, jnp.int32)
    thr, _ = lax.fori_loop(0, 31, search, (lo0, hi0))
    ge = jnp.where(a >= pltpu.bitcast(thr, F32), 1.0, 0.0)
    gt = jnp.where(a >= pltpu.bitcast(thr + 1, F32), 1.0, 0.0)
    eq = ge - gt
    need = cap - jnp.sum(gt, axis=1, keepdims=True)
    sel = gt + eq * jnp.where(_prefix_count(eq, t) < need, 1.0, 0.0)
    pos = jnp.where(sel > 0.0, _prefix_count(sel, t), -1.0)

    nrow = cap // SLOT_COLS
    prow = jnp.floor(pos * (1.0 / SLOT_COLS))
    pcol = pos - prow * SLOT_COLS
    tok = lax.broadcasted_iota(jnp.int32, (1, t), 1)
    a1 = a.astype(BF16)
    d1 = a - a1.astype(F32)
    a2 = d1.astype(BF16)
    a3 = (d1 - a2.astype(F32)).astype(BF16)
    t_hi = (tok >> 6).astype(F32)
    t_lo = (tok & 63).astype(F32)
    r_id = lax.broadcasted_iota(jnp.int32, (nrow, 1), 0).astype(F32)
    c_id = lax.broadcasted_iota(jnp.int32, (SLOT_COLS, 1), 0).astype(F32)
    for e in range(ne):
        in_row = prow[e:e + 1] == r_id
        vals = (t_hi, t_lo, a1[e:e + 1].astype(F32), a2[e:e + 1].astype(F32), a3[e:e + 1].astype(F32))
        lhs = jnp.concatenate([jnp.where(in_row, v, 0.0).astype(BF16) for v in vals], axis=0)
        rhs = jnp.where(pcol[e:e + 1] == c_id, 1.0, 0.0).astype(BF16)
        res = lax.dot_general(lhs, rhs, (((1,), (1,)), ((), ())), preferred_element_type=F32)
        idx_ref[0, e] = (res[0:nrow] * 64.0 + res[nrow:2 * nrow]).astype(jnp.int32)
        gate_ref[0, e] = res[2 * nrow:3 * nrow] + res[3 * nrow:4 * nrow] + res[4 * nrow:5 * nrow]


def _select(aff, cap):
    bsz, ne, t = aff.shape
    nrow = cap // SLOT_COLS
    ospec = pl.BlockSpec((1, ne, nrow, SLOT_COLS), lambda b: (b, 0, 0, 0))
    idx, gate = pl.pallas_call(
        functools.partial(_select_kernel, t=t, cap=cap),
        out_shape=(jax.ShapeDtypeStruct((bsz, ne, nrow, SLOT_COLS), jnp.int32),
                   jax.ShapeDtypeStruct((bsz, ne, nrow, SLOT_COLS), F32)),
        grid=(bsz,),
        in_specs=[pl.BlockSpec((1, ne, t), lambda b: (b, 0, 0))],
        out_specs=(ospec, ospec),
        compiler_params=_cparams("arbitrary"),
        name="select",
    )(aff)
    return idx.reshape(bsz, ne, cap), gate.reshape(bsz, ne, cap)


def _experts_kernel(idx_ref, idxn_ref, h_hbm, w13_ref, w2_ref, y_ref, w13s, w2s, xe, sem, fence,
                    *, ngroups, bg, t, cap, nsteps):
    e = pl.program_id(0)
    g = pl.program_id(1)
    step = e * ngroups + g
    slot = step % 2
    f = w2_ref.shape[2]
    fc = min(f, EXPERT_FCHUNK)
    nck = f // fc
    m = bg * cap
    if nck > 2:
        first = m // 8
        bounds = [0] + [first + (m - first) * i // (nck - 2) for i in range(nck - 1)]
    else:
        bounds = [0, m]

    def row_copy(src_row, dst_slot, j):
        return pltpu.make_async_copy(h_hbm.at[pl.ds(pl.multiple_of(src_row * TILE_ROWS, TILE_ROWS), TILE_ROWS), :],
                                     xe.at[dst_slot, pl.ds(j * TILE_ROWS, TILE_ROWS), :], sem.at[dst_slot])

    @pl.when(step == 0)
    def _():
        for bi in range(bg):
            def body(j, carry):
                row_copy((g * bg + bi) * t + idx_ref[0, bi, j], slot, bi * cap + j).start()
                return carry
            lax.fori_loop(0, cap, body, 0, unroll=8)

    @pl.when(g == 0)
    def _():
        w13s[...] = w13_ref[0, 0].astype(BF16)
        w2s[...] = w2_ref[0, 0].astype(BF16)

    def wait_body(j, carry):
        row_copy(0, slot, j).wait()
        return carry
    lax.fori_loop(0, m, wait_body, 0, unroll=8)

    last = step + 1 >= nsteps
    nbase = jnp.where(last, g, (step + 1) % ngroups) * (bg * t)
    xb = _load_token_tiles(xe.at[slot], m).astype(BF16)
    y = None
    for k in range(nck):
        a = jnp.dot(xb, w13s[:, k * fc:(k + 1) * fc], preferred_element_type=F32)
        gate = jnp.dot(xb, w13s[:, f + k * fc:f + (k + 1) * fc], preferred_element_type=F32)
        hid = (a * _sigmoid(a) * gate).astype(BF16)
        yk = jnp.dot(hid, w2s[k * fc:(k + 1) * fc, :], preferred_element_type=F32)
        y = yk if y is None else y + yk
        if k < len(bounds) - 1:
            for j in range(bounds[k], bounds[k + 1]):
                row_copy(nbase + (j // cap) * t + idxn_ref[0, j // cap, j % cap], 1 - slot, j).start()
            pl.semaphore_signal(fence, 1)
            pl.semaphore_wait(fence, 1)
    for bi in range(bg):
        _store_token_tiles(y_ref.at[bi, 0], y[bi * cap:(bi + 1) * cap])

    @pl.when(last)
    def _():
        def drain(j, carry):
            row_copy(0, 1 - slot, j).wait()
            return carry
        lax.fori_loop(0, m, drain, 0, unroll=8)


def _experts(h2, idx, w13, w2, layer):
    bsz = h2.shape[0]
    t = h2.shape[1] // TILE_ROWS
    d = D_MODEL
    ne, cap = idx.shape[1], idx.shape[2]
    f = w2.shape[2]
    bg = max(1, min(bsz, EXPERT_MIN_ROWS // cap))
    ngroups = bsz // bg
    nsteps = ne * ngroups
    idx3 = idx.transpose(1, 0, 2).reshape(nsteps, bg, cap)

    def nxt(e, g):
        return (jnp.minimum(e * ngroups + g + 1, nsteps - 1), 0, 0)

    return pl.pallas_call(
        functools.partial(_experts_kernel, ngroups=ngroups, bg=bg, t=t, cap=cap, nsteps=nsteps),
        out_shape=jax.ShapeDtypeStruct((bsz, ne, cap * TILE_ROWS, LANES), F32),
        grid=(ne, ngroups),
        in_specs=[
            pl.BlockSpec((1, bg, cap), lambda e, g: (e * ngroups + g, 0, 0), memory_space=pltpu.SMEM),
            pl.BlockSpec((1, bg, cap), nxt, memory_space=pltpu.SMEM),
            pl.BlockSpec(memory_space=pl.ANY),
            pl.BlockSpec((1, 1, d, 2 * f), lambda e, g: (layer, e, 0, 0)),
            pl.BlockSpec((1, 1, f, d), lambda e, g: (layer, e, 0, 0)),
        ],
        out_specs=pl.BlockSpec((bg, 1, cap * TILE_ROWS, LANES), lambda e, g: (g, e, 0, 0)),
        scratch_shapes=[pltpu.VMEM((d, 2 * f), BF16), pltpu.VMEM((f, d), BF16),
                        pltpu.VMEM((2, bg * cap * TILE_ROWS, LANES), F32), pltpu.SemaphoreType.DMA((2,)),
                        pltpu.SemaphoreType.REGULAR],
        compiler_params=_cparams("arbitrary", "arbitrary"),
        name="experts",
    )(idx3, idx3, h2.reshape(bsz * t * TILE_ROWS, LANES), w13, w2)


def _combine_kernel(idx_ref, gate_ref, y_ref, x_ref, g_ref, nw_ref, o_ref, m_scr, *, cap, eg, nsc, tm):
    step = pl.program_id(1)

    @pl.when(step == 0)
    def _():
        m_scr[...] = jnp.zeros_like(m_scr)

    group = 8

    @pl.when(step < nsc)
    def _():
        for ei in range(eg):
            def body(jj, carry):
                base = jj * group
                dst = [pl.ds(pl.multiple_of(idx_ref[ei, 0, base + r] * TILE_ROWS, TILE_ROWS), TILE_ROWS)
                       for r in range(group)]
                acc = [m_scr[d, :] for d in dst]
                for r in range(group):
                    src = pl.ds(pl.multiple_of((base + r) * TILE_ROWS, TILE_ROWS), TILE_ROWS)
                    m_scr[dst[r], :] = acc[r] + gate_ref[ei, 0, base + r] * y_ref[0, ei, src, :]
                return carry
            lax.fori_loop(0, cap // group, body, 0, unroll=2)

    @pl.when(step >= nsc)
    def _():
        base = pl.multiple_of((step - nsc) * (tm * TILE_ROWS), tm * TILE_ROWS)
        m = jnp.concatenate([m_scr[pl.ds(base + k, tm, stride=TILE_ROWS), :] for k in range(TILE_ROWS)], axis=1)
        o_ref[0] = x_ref[0] + g_ref[0] * _rms(m, nw_ref[...])


def _combine(y, idx, gate, x, g2, nw):
    bsz, t, d = x.shape
    _, ne, cap = idx.shape
    eg = max(1, min(ne, COMBINE_MIN_ROWS // cap))
    nsc = ne // eg
    tm = min(t, 512)
    idx3 = idx.reshape(bsz * ne, 1, cap)
    gate3 = gate.reshape(bsz * ne, 1, cap)
    sc = lambda s: jnp.minimum(s, nsc - 1)
    rt = lambda s: jnp.maximum(s - nsc, 0)
    sspec = pl.BlockSpec((eg, 1, cap), lambda b, s: (b * nsc + sc(s), 0, 0), memory_space=pltpu.SMEM)
    tok = pl.BlockSpec((1, tm, d), lambda b, s: (b, rt(s), 0))
    return pl.pallas_call(
        functools.partial(_combine_kernel, cap=cap, eg=eg, nsc=nsc, tm=tm),
        out_shape=jax.ShapeDtypeStruct((bsz, t, d), F32),
        grid=(bsz, nsc + t // tm),
        in_specs=[sspec, sspec,
                  pl.BlockSpec((1, eg, cap * TILE_ROWS, LANES), lambda b, s: (b, sc(s), 0, 0)),
                  tok, pl.BlockSpec((1, 1, d), lambda b, s: (b, 0, 0)), pl.BlockSpec((1, d), lambda b, s: (0, 0))],
        out_specs=tok,
        scratch_shapes=[pltpu.VMEM((t * TILE_ROWS, LANES), F32)],
        compiler_params=_cparams("arbitrary", "arbitrary"),
        name="combine",
    )(idx3, gate3, y, x, g2, nw.reshape(1, d))


def _token_mix_and_moe(stream, z, hf, hb, dft, cdsd, mods, lw):
    sh2, sc2, g1, g2 = mods
    t = stream.shape[1]
    fo = _fourier(z, *dft, *cdsd)
    mid, h2, aff = _merge(z, hf, hb, fo, stream, lw["gnw"], lw["sgu_w"], lw["sgu_bias"], lw["snw"], lw["wb"],
                          lw["wo"], lw["nw1"], lw["nw2"], g1, sh2, sc2, lw["rw_t"], lw["layer"])
    cap = CAPACITY_FACTOR * t // N_EXPERTS
    idx, gate = _select(aff, cap)
    y = _experts(h2, idx, lw["w13"], lw["w2"], lw["layer"])
    return _combine(y, idx, gate, mid, g2, lw["nw3"])


def kernel(x, c, ctx, c_ctx, ada_w, ada_b, norm_w, w_in, hgrn_lb_raw, hgrn_gnorm_w, sgu_w, sgu_b, sgu_norm_w,
           w_branch, w_out, router_w, exp_w13, exp_w2):
    bsz, t, d = x.shape
    tc = ctx.shape[1]
    depth = ada_w.shape[0]
    assert d == D_MODEL and bsz + 1 <= MOD_ROWS and w_in.shape[2] == IN_COLS

    cvec = jnp.zeros((MOD_ROWS, d), F32).at[:bsz].set(c).at[bsz].set(c_ctx)
    mod = _adaln(cvec, ada_w, ada_b).reshape(depth, MOD_ROWS, 6, d)
    dft_lat, dft_ctx = _dft_tables(t), _dft_tables(tc)
    cdsd = _channel_dft()
    zero_state = jnp.zeros((bsz, HEADS, HEAD_DIM, HEAD_DIM), F32)
    w_in_b, w_branch_b, w_out_b = w_in.astype(BF16), w_branch.astype(BF16), w_out.astype(BF16)

    for l in range(depth):
        last = l == depth - 1
        lat = [mod[l, :bsz, k][:, None, :] for k in range(6)]
        cm = [jnp.broadcast_to(mod[l, bsz, k][None, None, :], (bsz, 1, d)) for k in range(6)]
        lw = dict(
            gnw=hgrn_gnorm_w[l].reshape(1, HEAD_DIM),
            sgu_w=sgu_w[l].astype(BF16),
            sgu_bias=jnp.broadcast_to(sgu_b[l][:, :, None], (HEADS, SGU_CHUNK, HEAD_DIM)),
            snw=sgu_norm_w[l],
            wb=w_branch_b, wo=w_out_b, layer=l,
            nw1=norm_w[l, 1].reshape(1, d), nw2=norm_w[l, 2].reshape(1, d), nw3=norm_w[l, 3],
            rw_t=router_w[l].T,
            w13=exp_w13, w2=exp_w2,
        )
        lb_raw = hgrn_lb_raw
        z_ctx = _inproj(ctx, norm_w[l, 0], cm[0], cm[1], w_in_b, l, A_IN if last else IN_COLS)
        hf_c, hb_c, s_f, s_b = _hgrn(z_ctx, lb_raw, zero_state, zero_state, l)
        z_lat = _inproj(x, norm_w[l, 0], lat[0], lat[1], w_in_b, l, IN_COLS)
        hf, hb, _, _ = _hgrn(z_lat, lb_raw, s_f, s_b, l)
        x = _token_mix_and_moe(x, z_lat, hf, hb, dft_lat, cdsd, (lat[3], lat[4], lat[2], lat[5]), lw)
        if not last:
            ctx = _token_mix_and_moe(ctx, z_ctx, hf_c, hb_c, dft_ctx, cdsd, (cm[3], cm[4], cm[2], cm[5]), lw)
    return x
```

```python
import functools
import math

import numpy as np
import jax
import jax.numpy as jnp
from jax import lax
from jax.experimental import pallas as pl
from jax.experimental.pallas import tpu as pltpu

F32 = jnp.float32
BF16 = jnp.bfloat16

D_MODEL = 1024
HEADS = 4
HEAD_DIM = 128
WIDTH = HEADS * HEAD_DIM
SGU_CHUNK = 128
N_BRANCH = 3
N_EXPERTS = 16
CAPACITY_FACTOR = 2
EPS = 1e-6
LB_FLOOR = 1e-20

COL_Q, COL_FF, COL_FB, COL_I = 0, WIDTH, 2 * WIDTH, 3 * WIDTH
COL_GA = 4 * WIDTH
COL_FOURIER = 5 * WIDTH
COL_U = 6 * WIDTH
COL_V = 7 * WIDTH
COL_GATES = 8 * WIDTH
IN_COLS = COL_GATES + N_BRANCH * D_MODEL
A_IN = 4 * WIDTH

HGRN_CHUNK = 64
HGRN_SUB = 16
HGRN_BLOCK = 256
EXP_CLAMP = 80.0
LOG2E = 1.4426950408889634
EXP2_CLAMP = 115.0

VMEM_LIMIT = 56 * 1024 * 1024
EXPERT_FCHUNK = 256
COMBINE_MIN_ROWS = 256
EXPERT_MIN_ROWS = 1024
SLOT_COLS = 32
DFX_ROWS = 16
TILE_ROWS, LANES = 8, 128
MOD_ROWS = 16


def _cparams(*sem):
    return pltpu.CompilerParams(dimension_semantics=sem, vmem_limit_bytes=VMEM_LIMIT)


def _sigmoid(v):
    return 1.0 / (1.0 + jnp.exp2(v * (-LOG2E)))


def _rms(v, w):
    return v * lax.rsqrt(jnp.mean(v * v, axis=-1, keepdims=True) + EPS) * w


def _gelu(v):
    return 0.5 * v * (1.0 + lax.erf(v * (1.0 / math.sqrt(2.0))))


def _store_token_tiles(ref, val):
    rows = val.shape[0]
    for k in range(TILE_ROWS):
        ref[pl.ds(k, rows, stride=TILE_ROWS), :] = val[:, k * LANES:(k + 1) * LANES]


def _load_token_tiles(ref, rows):
    return jnp.concatenate([ref[pl.ds(k, rows, stride=TILE_ROWS), :] for k in range(TILE_ROWS)], axis=1)


def _adaln_kernel(c_ref, w_ref, b_ref, o_ref):
    c = c_ref[...]
    s = c * _sigmoid(c)
    o_ref[0] = jnp.dot(s, w_ref[0], precision=lax.Precision.HIGHEST, preferred_element_type=F32) + b_ref[0]


def _adaln(cvec, ada_w, ada_b):
    depth, d, n = ada_w.shape
    tn = 512
    return pl.pallas_call(
        _adaln_kernel,
        out_shape=jax.ShapeDtypeStruct((depth, MOD_ROWS, n), F32),
        grid=(depth, n // tn),
        in_specs=[
            pl.BlockSpec((MOD_ROWS, d), lambda l, j: (0, 0)),
            pl.BlockSpec((1, d, tn), lambda l, j: (l, 0, j)),
            pl.BlockSpec((1, 1, tn), lambda l, j: (l, 0, j)),
        ],
        out_specs=pl.BlockSpec((1, MOD_ROWS, tn), lambda l, j: (l, 0, j)),
        compiler_params=_cparams("arbitrary", "arbitrary"),
        name="adaln",
    )(cvec, ada_w, ada_b.reshape(depth, 1, n))


def _inproj_kernel(x_ref, nw_ref, sh_ref, sc_ref, w_ref, z_ref, *, tn):
    h = (_rms(x_ref[0], nw_ref[...]) * (1.0 + sc_ref[0]) + sh_ref[0]).astype(BF16)
    for j in range(w_ref.shape[2] // tn):
        z_ref[0, :, j * tn:(j + 1) * tn] = jnp.dot(
            h, w_ref[0, :, j * tn:(j + 1) * tn], preferred_element_type=F32).astype(z_ref.dtype)


def _inproj(x, nw, sh, sc, w, layer, n):
    bsz, t, d = x.shape
    tm = min(t, 512)
    return pl.pallas_call(
        functools.partial(_inproj_kernel, tn=1024),
        out_shape=jax.ShapeDtypeStruct((bsz, t, n), BF16),
        grid=(bsz, t // tm),
        in_specs=[
            pl.BlockSpec((1, tm, d), lambda b, i: (b, i, 0)),
            pl.BlockSpec((1, d), lambda b, i: (0, 0)),
            pl.BlockSpec((1, 1, d), lambda b, i: (b, 0, 0)),
            pl.BlockSpec((1, 1, d), lambda b, i: (b, 0, 0)),
            pl.BlockSpec((1, d, n), lambda b, i: (layer, 0, 0), pipeline_mode=pl.Buffered(1)),
        ],
        out_specs=pl.BlockSpec((1, tm, n), lambda b, i: (b, i, 0)),
        compiler_params=_cparams("arbitrary", "arbitrary"),
        name="inproj",
    )(x, nw.reshape(1, d), sh, sc, w)


def _lower_bound(raw, layer):
    m = jnp.max(raw, axis=0, keepdims=True)
    e = jnp.exp(raw - m)
    p = e / jnp.sum(e, axis=0, keepdims=True)
    lb = jnp.zeros_like(p[0:1])
    for j in range(1, layer + 1):
        lb = lb + p[j:j + 1]
    return lb


def _hgrn_block(zq, zf, zi, lb, st_ref, reverse):
    tb = zq.shape[0]
    c, sc = HGRN_CHUNK, HGRN_SUB
    ns = c // sc
    zq = zq.astype(F32)
    zf = zf.astype(F32)
    q = zq * _sigmoid(zq)
    e = jnp.exp2(jnp.maximum(zf, -EXP_CLAMP) * (-LOG2E))
    r = 1.0 / (1.0 + e)
    lbm = jnp.maximum(lb, LB_FLOOR)
    f = lbm + (1.0 - lb) * r
    kk = (1.0 - lb) * (e * r) - (lbm - lb)
    lf = jnp.log2(f)
    v = zi.astype(BF16)

    row = lax.broadcasted_iota(jnp.int32, (tb, tb), 0)
    col = lax.broadcasted_iota(jnp.int32, (tb, tb), 1)
    absorbed = (col >= row) if reverse else (col <= row)
    trib = jnp.where(absorbed & ((row // c) == (col // c)), 1.0, 0.0).astype(BF16)
    l1 = lf.astype(BF16)
    l2 = (lf - l1.astype(F32)).astype(BF16)
    b = jnp.dot(trib, l1, preferred_element_type=F32) + jnp.dot(trib, l2, preferred_element_type=F32)
    b_excl = b - lf

    refs = []
    for s in range(tb // sc):
        first = s * sc + (sc - 1 if reverse else 0)
        refs.append(b_excl[first:first + 1])
    r_own = jnp.concatenate([jnp.broadcast_to(x, (sc, WIDTH)) for x in refs], axis=0)
    q_own = q * jnp.exp2(b - r_own)
    k_own = (kk * jnp.exp2(jnp.minimum(r_own - b, EXP2_CLAMP))).astype(BF16)
    zero_tile = jnp.zeros((sc, HEAD_DIM), BF16)
    ti = lax.broadcasted_iota(jnp.int32, (c, c), 0)
    si = lax.broadcasted_iota(jnp.int32, (c, c), 1)
    tri_c = (si >= ti) if reverse else (si <= ti)

    nch = tb // c
    order = list(range(nch - 1, -1, -1)) if reverse else list(range(nch))
    qe_c, g_c, sc_c, v_c, kd_c = {}, {}, {}, {}, {}
    for ch in order:
        lo = ch * c
        b_c = b[lo:lo + c]
        b_end = b_c[0:1] if reverse else b_c[c - 1:c]
        qe_c[ch] = (q[lo:lo + c] * jnp.exp2(b_c)).astype(BF16)
        kd_c[ch] = (kk[lo:lo + c] * jnp.exp2(b_end - b_c)).astype(BF16)
        g_c[ch] = jnp.exp2(b_end)
        v_c[ch] = v[lo:lo + c]
        qij = {}
        for i in range(ns):
            for j in range(ns):
                if (j >= i) if reverse else (j <= i):
                    gi = jnp.exp2(refs[ch * ns + i] - refs[ch * ns + j])
                    qij[i, j] = (q_own[lo + i * sc:lo + (i + 1) * sc] * gi).astype(BF16)
        for h in range(HEADS):
            sl = slice(h * HEAD_DIM, (h + 1) * HEAD_DIM)
            q_aug = jnp.concatenate([
                jnp.concatenate([qij[i, j][:, sl] if (i, j) in qij else zero_tile for j in range(ns)], axis=1)
                for i in range(ns)], axis=0)
            k_aug = jnp.concatenate([
                jnp.concatenate([k_own[lo + i * sc:lo + (i + 1) * sc, sl] if j == i else zero_tile
                                 for j in range(ns)], axis=1)
                for i in range(ns)], axis=0)
            scores = lax.dot_general(q_aug, k_aug, (((1,), (1,)), ((), ())), preferred_element_type=F32)
            sc_c[ch, h] = jnp.where(tri_c, scores, 0.0).astype(BF16)
    o_c, upd_c = {}, {}
    for ch in order:
        for h in range(HEADS):
            sl = slice(h * HEAD_DIM, (h + 1) * HEAD_DIM)
            o_c[ch, h] = jnp.dot(sc_c[ch, h], v_c[ch][:, sl], preferred_element_type=F32)
            upd_c[ch, h] = lax.dot_general(v_c[ch][:, sl], kd_c[ch][:, sl], (((0,), (0,)), ((), ())),
                                           preferred_element_type=F32)
    for h in range(HEADS):
        sl = slice(h * HEAD_DIM, (h + 1) * HEAD_DIM)
        st = st_ref[h]
        for ch in order:
            o_c[ch, h] = o_c[ch, h] + lax.dot_general(qe_c[ch][:, sl], st.astype(BF16), (((1,), (1,)), ((), ())),
                                                      preferred_element_type=F32)
            st = st * g_c[ch][:, sl] + upd_c[ch, h]
        st_ref[h] = st
    return jnp.concatenate([jnp.concatenate([o_c[ch, h] for h in range(HEADS)], axis=1) for ch in range(nch)], axis=0)


def _hgrn_kernel(zqf, zff, zif, zqb, zfb, zib, lbf_ref, lbb_ref, s0f, s0b,
                 of_ref, ob_ref, sf_ref, sb_ref, st_scr, *, layer, nsteps):
    n = pl.program_id(1)

    @pl.when(n == 0)
    def _():
        st_scr[0] = s0f[0]
        st_scr[1] = s0b[0]

    lb_f = _lower_bound(lbf_ref[...], layer)
    lb_b = _lower_bound(lbb_ref[...], layer)
    of_ref[0] = _hgrn_block(zqf[0], zff[0], zif[0], lb_f, st_scr.at[0], False).astype(of_ref.dtype)
    ob_ref[0] = _hgrn_block(zqb[0], zfb[0], zib[0], lb_b, st_scr.at[1], True).astype(ob_ref.dtype)

    @pl.when(n == nsteps - 1)
    def _():
        sf_ref[0] = st_scr[0]
        sb_ref[0] = st_scr[1]


def _hgrn(z, lb_raw, s0f, s0b, layer):
    bsz, t, _ = z.shape
    c = min(t, HGRN_BLOCK)
    n = t // c
    depth = lb_raw.shape[1]
    zspec = lambda colblk, rev: pl.BlockSpec(
        (1, c, WIDTH), (lambda b, i: (b, n - 1 - i, colblk)) if rev else (lambda b, i: (b, i, colblk)))
    sspec = pl.BlockSpec((1, HEADS, HEAD_DIM, HEAD_DIM), lambda b, i: (b, 0, 0, 0))
    ospec = lambda rev: pl.BlockSpec((1, c, WIDTH), (lambda b, i: (b, n - 1 - i, 0)) if rev else (lambda b, i: (b, i, 0)))
    st_shape = jax.ShapeDtypeStruct((bsz, HEADS, HEAD_DIM, HEAD_DIM), F32)
    return pl.pallas_call(
        functools.partial(_hgrn_kernel, layer=layer, nsteps=n),
        out_shape=(jax.ShapeDtypeStruct((bsz, t, WIDTH), BF16), jax.ShapeDtypeStruct((bsz, t, WIDTH), BF16),
                   st_shape, st_shape),
        grid=(bsz, n),
        in_specs=[
            zspec(COL_Q // WIDTH, False), zspec(COL_FF // WIDTH, False), zspec(COL_I // WIDTH, False),
            zspec(COL_Q // WIDTH, True), zspec(COL_FB // WIDTH, True), zspec(COL_I // WIDTH, True),
            pl.BlockSpec((depth, WIDTH), lambda b, i: (0, 0)),
            pl.BlockSpec((depth, WIDTH), lambda b, i: (0, 0)),
            sspec, sspec,
        ],
        out_specs=(ospec(False), ospec(True), sspec, sspec),
        scratch_shapes=[pltpu.VMEM((2, HEADS, HEAD_DIM, HEAD_DIM), F32)],
        compiler_params=_cparams("arbitrary", "arbitrary"),
        name="hgrn",
    )(z, z, z, z, z, z, lb_raw[0], lb_raw[1], s0f, s0b)


def _fourier_kernel(z_ref, dft_ref, dfx_ref, cd_ref, sd_ref, o_ref, r_scr, *, t, tp, scale):
    @pl.when(pl.program_id(1) == 0)
    def _():
        for g in range(HEADS):
            zg = z_ref[0, :, g * HEAD_DIM:(g + 1) * HEAD_DIM]
            r_scr[0:t, g * HEAD_DIM:(g + 1) * HEAD_DIM] = jnp.dot(
                zg, cd_ref[...], preferred_element_type=F32).astype(BF16)
            r_scr[t:2 * t, g * HEAD_DIM:(g + 1) * HEAD_DIM] = (-jnp.dot(
                zg, sd_ref[...], preferred_element_type=F32)).astype(BF16)

    ca = jnp.dot(dft_ref[:, 0:t], r_scr[0:t, :], preferred_element_type=F32)
    sb = jnp.dot(dft_ref[:, t:2 * t], r_scr[t:2 * t, :], preferred_element_type=F32)
    o_ref[0, 0] = ((ca + sb) * scale).astype(o_ref.dtype)
    xa = jnp.dot(dfx_ref[:, 0:t], r_scr[0:t, :], preferred_element_type=F32)
    xs = jnp.dot(dfx_ref[:, t:2 * t], r_scr[t:2 * t, :], preferred_element_type=F32)
    src = jnp.concatenate([((ca - sb) * scale).astype(BF16), ((xa - xs) * scale).astype(BF16)], axis=0)
    rr = lax.broadcasted_iota(jnp.int32, (tp, tp + DFX_ROWS), 0)
    cc = lax.broadcasted_iota(jnp.int32, (tp, tp + DFX_ROWS), 1)
    flip = jnp.where(cc == tp - rr, 1.0, 0.0).astype(BF16)
    o_ref[0, 1] = jnp.dot(flip, src, preferred_element_type=F32).astype(o_ref.dtype)


def _fourier_tile(t):
    return min(t // 2, 512)


def _fourier(z, dft, dfx, cd, sd):
    bsz, t, _ = z.shape
    half = t // 2
    tp = _fourier_tile(t)
    nt = half // tp
    return pl.pallas_call(
        functools.partial(_fourier_kernel, t=t, tp=tp, scale=1.0 / math.sqrt(t * HEAD_DIM)),
        out_shape=jax.ShapeDtypeStruct((bsz, 2, half, WIDTH), BF16),
        grid=(bsz, nt),
        in_specs=[
            pl.BlockSpec((1, t, WIDTH), lambda b, i: (b, 0, COL_FOURIER // WIDTH)),
            pl.BlockSpec((tp, 2 * t), lambda b, i: (i, 0)),
            pl.BlockSpec((DFX_ROWS, 2 * t), lambda b, i: (i, 0)),
            pl.BlockSpec((HEAD_DIM, HEAD_DIM), lambda b, i: (0, 0)),
            pl.BlockSpec((HEAD_DIM, HEAD_DIM), lambda b, i: (0, 0)),
        ],
        out_specs=pl.BlockSpec((1, 2, tp, WIDTH), lambda b, i: (b, 0, i, 0)),
        scratch_shapes=[pltpu.VMEM((2 * t, WIDTH), BF16)],
        compiler_params=_cparams("arbitrary", "arbitrary"),
        name="fourier",
    )(z, dft, dfx, cd, sd)


def _dft_tables(t):
    r = int(round(math.sqrt(t)))
    assert r * r == t
    half = t // 2
    tp = _fourier_tile(t)
    s = np.arange(t, dtype=np.int64)
    a_ang = 2.0 * np.pi * ((np.arange(r // 2)[:, None] * r * s[None, :]) % t) / t
    b_ang = 2.0 * np.pi * ((np.arange(r)[:, None] * s[None, :]) % t) / t
    ac, asn = jnp.asarray(np.cos(a_ang), F32)[:, None, :], jnp.asarray(np.sin(a_ang), F32)[:, None, :]
    bc, bsn = jnp.asarray(np.cos(b_ang), F32)[None, :, :], jnp.asarray(np.sin(b_ang), F32)[None, :, :]
    ct = (ac * bc - asn * bsn).reshape(half, t)
    st = (asn * bc + ac * bsn).reshape(half, t)
    x_ang = 2.0 * np.pi * ((np.arange(tp, half + 1, tp)[:, None] * s[None, :]) % t) / t
    xrow = np.zeros((half // tp, DFX_ROWS, 2 * t), np.float32)
    xrow[:, 0, :t] = np.cos(x_ang)
    xrow[:, 0, t:] = np.sin(x_ang)
    dfx = jnp.asarray(xrow.reshape(-1, 2 * t), F32).astype(BF16)
    return jnp.concatenate([ct, st], axis=1).astype(BF16), dfx


def _channel_dft():
    d = np.arange(HEAD_DIM)
    ang = 2.0 * np.pi * ((d[:, None] * d[None, :]) % HEAD_DIM) / HEAD_DIM
    return jnp.asarray(np.cos(ang), F32).astype(BF16), jnp.asarray(np.sin(ang), F32).astype(BF16)


def _merge_kernel(hf_ref, hb_ref, ga_ref, fo_ref, u_ref, v_ref, g0_ref, g1_ref, g2_ref, x_ref,
                  gnw_ref, sw_ref, sb_ref, snw_ref, wb_ref, wo_ref, nw1_ref, nw2_ref,
                  gate1_ref, sh2_ref, sc2_ref, rw_ref,
                  xo_ref, h2_ref, aff_ref):
    tm = x_ref.shape[1]
    hsum = hf_ref[0].astype(F32) + hb_ref[0].astype(F32)
    ga = ga_ref[0].astype(F32)
    oa = []
    for h in range(HEADS):
        sl = slice(h * HEAD_DIM, (h + 1) * HEAD_DIM)
        oa.append(_rms(hsum[:, sl], gnw_ref[...]))
    o_a = (jnp.concatenate(oa, axis=1) * (ga * _sigmoid(ga))).astype(BF16)
    u = _gelu(u_ref[0].astype(F32))
    vv = _gelu(v_ref[0].astype(F32))
    rows = []
    for ch in range(tm // SGU_CHUNK):
        rs = slice(ch * SGU_CHUNK, (ch + 1) * SGU_CHUNK)
        cols = []
        for g in range(HEADS):
            sl = slice(g * HEAD_DIM, (g + 1) * HEAD_DIM)
            vn = _rms(vv[rs, sl], snw_ref[g:g + 1, :]).astype(BF16)
            cols.append(jnp.dot(sw_ref[g], vn, preferred_element_type=F32) + sb_ref[g])
        rows.append(jnp.concatenate(cols, axis=1))
    o_c = (u * jnp.concatenate(rows, axis=0)).astype(BF16)
    merged = _sigmoid(g0_ref[0].astype(F32)) * jnp.dot(o_a, wb_ref[0, 0], preferred_element_type=F32)
    merged += _sigmoid(g1_ref[0].astype(F32)) * jnp.dot(fo_ref[0, 0], wb_ref[0, 1], preferred_element_type=F32)
    merged += _sigmoid(g2_ref[0].astype(F32)) * jnp.dot(o_c, wb_ref[0, 2], preferred_element_type=F32)
    y = jnp.dot(merged.astype(BF16), wo_ref[0], preferred_element_type=F32)
    xn = x_ref[0] + gate1_ref[0] * _rms(y, nw1_ref[...])
    xo_ref[0] = xn
    h2 = _rms(xn, nw2_ref[...]) * (1.0 + sc2_ref[0]) + sh2_ref[0]
    _store_token_tiles(h2_ref.at[0], h2)
    h_hi = h2.astype(BF16)
    h_lo = (h2 - h_hi.astype(F32)).astype(BF16)
    rw = rw_ref[...]
    r_hi = rw.astype(BF16)
    r_lo = (rw - r_hi.astype(F32)).astype(BF16)
    nt = (((1,), (1,)), ((), ()))
    logits = (lax.dot_general(r_hi, h_hi, nt, preferred_element_type=F32)
              + lax.dot_general(r_hi, h_lo, nt, preferred_element_type=F32)
              + lax.dot_general(r_lo, h_hi, nt, preferred_element_type=F32))
    ex = jnp.exp(logits - jnp.max(logits, axis=0, keepdims=True))
    aff_ref[0] = ex / jnp.sum(ex, axis=0, keepdims=True)


def _merge(z, hf, hb, fo, x, gnw, sgu_w, sgu_bias, snw, wb, wo, nw1, nw2, gate1, sh2, sc2, rw_t, layer):
    bsz, t, d = x.shape
    tm = _fourier_tile(t)
    nt = t // (2 * tm)
    zw = lambda colblk: pl.BlockSpec((1, tm, WIDTH), lambda b, i: (b, i, colblk))
    zd = lambda colblk: pl.BlockSpec((1, tm, d), lambda b, i: (b, i, colblk))
    tok = lambda width: pl.BlockSpec((1, tm, width), lambda b, i: (b, i, 0))
    full = lambda *shape: pl.BlockSpec(shape, lambda b, i: (0,) * len(shape))
    mod = pl.BlockSpec((1, 1, d), lambda b, i: (b, 0, 0))
    return pl.pallas_call(
        _merge_kernel,
        out_shape=(jax.ShapeDtypeStruct((bsz, t, d), F32), jax.ShapeDtypeStruct((bsz, t * TILE_ROWS, LANES), F32),
                   jax.ShapeDtypeStruct((bsz, N_EXPERTS, t), F32)),
        grid=(bsz, t // tm),
        in_specs=[
            tok(WIDTH), tok(WIDTH), zw(COL_GA // WIDTH),
            pl.BlockSpec((1, 1, tm, WIDTH), lambda b, i: (b, i // nt, jnp.where(i < nt, i, 2 * nt - 1 - i), 0)),
            zw(COL_U // WIDTH), zw(COL_V // WIDTH),
            zd(COL_GATES // d), zd(COL_GATES // d + 1), zd(COL_GATES // d + 2), tok(d),
            full(1, HEAD_DIM), full(HEADS, SGU_CHUNK, SGU_CHUNK), full(HEADS, SGU_CHUNK, HEAD_DIM),
            full(HEADS, HEAD_DIM), pl.BlockSpec((1, N_BRANCH, WIDTH, d), lambda b, i: (layer, 0, 0, 0)),
            pl.BlockSpec((1, d, d), lambda b, i: (layer, 0, 0)), full(1, d), full(1, d),
            mod, mod, mod, full(N_EXPERTS, d),
        ],
        out_specs=(tok(d), pl.BlockSpec((1, tm * TILE_ROWS, LANES), lambda b, i: (b, i, 0)),
                   pl.BlockSpec((1, N_EXPERTS, tm), lambda b, i: (b, 0, i))),
        compiler_params=_cparams("arbitrary", "arbitrary"),
        name="merge",
    )(hf, hb, z, fo, z, z, z, z, z, x, gnw, sgu_w, sgu_bias, snw, wb, wo, nw1, nw2, gate1, sh2, sc2, rw_t)


def _prefix_count(mask, t):
    lanes = 128
    s_i = lax.broadcasted_iota(jnp.int32, (lanes, lanes), 0)
    t_i = lax.broadcasted_iota(jnp.int32, (lanes, lanes), 1)
    upper = jnp.where(s_i < t_i, 1.0, 0.0).astype(BF16)
    off = jnp.zeros((mask.shape[0], 1), F32)
    parts = []
    for blk in range(t // lanes):
        m = mask[:, blk * lanes:(blk + 1) * lanes]
        parts.append(jnp.dot(m.astype(BF16), upper, preferred_element_type=F32) + off)
        off = off + jnp.sum(m, axis=1, keepdims=True)
    return jnp.concatenate(parts, axis=1)


def _select_kernel(aff_ref, idx_ref, gate_ref, *, t, cap):
    a = aff_ref[0]
    ne = a.shape[0]

    def search(_, carry):
        lo, hi = carry
        mid = lo + ((hi - lo) >> 1)
        cnt = jnp.sum(jnp.where(a >= pltpu.bitcast(mid, F32), 1.0, 0.0), axis=1, keepdims=True)
        ok = cnt >= cap
        return jnp.where(ok, mid, lo), jnp.where(ok, hi, mid)

    lo0 = jnp.zeros((ne, 1), jnp.int32)
    hi0 = jnp.full((ne, 1), 0x7F800000, jnp.int32)
    thr, _ = lax.fori_loop(0, 31, search, (lo0, hi0))
    ge = jnp.where(a >= pltpu.bitcast(thr, F32), 1.0, 0.0)
    gt = jnp.where(a >= pltpu.bitcast(thr + 1, F32), 1.0, 0.0)
    eq = ge - gt
    need = cap - jnp.sum(gt, axis=1, keepdims=True)
    sel = gt + eq * jnp.where(_prefix_count(eq, t) < need, 1.0, 0.0)
    pos = jnp.where(sel > 0.0, _prefix_count(sel, t), -1.0)

    nrow = cap // SLOT_COLS
    prow = jnp.floor(pos * (1.0 / SLOT_COLS))
    pcol = pos - prow * SLOT_COLS
    tok = lax.broadcasted_iota(jnp.int32, (1, t), 1)
    a1 = a.astype(BF16)
    d1 = a - a1.astype(F32)
    a2 = d1.astype(BF16)
    a3 = (d1 - a2.astype(F32)).astype(BF16)
    t_hi = (tok >> 6).astype(F32)
    t_lo = (tok & 63).astype(F32)
    r_id = lax.broadcasted_iota(jnp.int32, (nrow, 1), 0).astype(F32)
    c_id = lax.broadcasted_iota(jnp.int32, (SLOT_COLS, 1), 0).astype(F32)
    for e in range(ne):
        in_row = prow[e:e + 1] == r_id
        vals = (t_hi, t_lo, a1[e:e + 1].astype(F32), a2[e:e + 1].astype(F32), a3[e:e + 1].astype(F32))
        lhs = jnp.concatenate([jnp.where(in_row, v, 0.0).astype(BF16) for v in vals], axis=0)
        rhs = jnp.where(pcol[e:e + 1] == c_id, 1.0, 0.0).astype(BF16)
        res = lax.dot_general(lhs, rhs, (((1,), (1,)), ((), ())), preferred_element_type=F32)
        idx_ref[0, e] = (res[0:nrow] * 64.0 + res[nrow:2 * nrow]).astype(jnp.int32)
        gate_ref[0, e] = res[2 * nrow:3 * nrow] + res[3 * nrow:4 * nrow] + res[4 * nrow:5 * nrow]


def _select(aff, cap):
    bsz, ne, t = aff.shape
    nrow = cap // SLOT_COLS
    ospec = pl.BlockSpec((1, ne, nrow, SLOT_COLS), lambda b: (b, 0, 0, 0))
    idx, gate = pl.pallas_call(
        functools.partial(_select_kernel, t=t, cap=cap),
        out_shape=(jax.ShapeDtypeStruct((bsz, ne, nrow, SLOT_COLS), jnp.int32),
                   jax.ShapeDtypeStruct((bsz, ne, nrow, SLOT_COLS), F32)),
        grid=(bsz,),
        in_specs=[pl.BlockSpec((1, ne, t), lambda b: (b, 0, 0))],
        out_specs=(ospec, ospec),
        compiler_params=_cparams("arbitrary"),
        name="select",
    )(aff)
    return idx.reshape(bsz, ne, cap), gate.reshape(bsz, ne, cap)


def _experts_kernel(idx_ref, idxn_ref, h_hbm, w13_ref, w2_ref, y_ref, w13s, w2s, xe, sem, fence,
                    *, ngroups, bg, t, cap, nsteps):
    e = pl.program_id(0)
    g = pl.program_id(1)
    step = e * ngroups + g
    slot = step % 2
    f = w2_ref.shape[2]
    fc = min(f, EXPERT_FCHUNK)
    nck = f // fc
    m = bg * cap
    if nck > 2:
        first = m // 8
        bounds = [0] + [first + (m - first) * i // (nck - 2) for i in range(nck - 1)]
    else:
        bounds = [0, m]

    def row_copy(src_row, dst_slot, j):
        return pltpu.make_async_copy(h_hbm.at[pl.ds(pl.multiple_of(src_row * TILE_ROWS, TILE_ROWS), TILE_ROWS), :],
                                     xe.at[dst_slot, pl.ds(j * TILE_ROWS, TILE_ROWS), :], sem.at[dst_slot])

    @pl.when(step == 0)
    def _():
        for bi in range(bg):
            def body(j, carry):
                row_copy((g * bg + bi) * t + idx_ref[0, bi, j], slot, bi * cap + j).start()
                return carry
            lax.fori_loop(0, cap, body, 0, unroll=8)

    @pl.when(g == 0)
    def _():
        w13s[...] = w13_ref[0, 0].astype(BF16)
        w2s[...] = w2_ref[0, 0].astype(BF16)

    def wait_body(j, carry):
        row_copy(0, slot, j).wait()
        return carry
    lax.fori_loop(0, m, wait_body, 0, unroll=8)

    last = step + 1 >= nsteps
    nbase = jnp.where(last, g, (step + 1) % ngroups) * (bg * t)
    xb = _load_token_tiles(xe.at[slot], m).astype(BF16)
    y = None
    for k in range(nck):
        a = jnp.dot(xb, w13s[:, k * fc:(k + 1) * fc], preferred_element_type=F32)
        gate = jnp.dot(xb, w13s[:, f + k * fc:f + (k + 1) * fc], preferred_element_type=F32)
        hid = (a * _sigmoid(a) * gate).astype(BF16)
        yk = jnp.dot(hid, w2s[k * fc:(k + 1) * fc, :], preferred_element_type=F32)
        y = yk if y is None else y + yk
        if k < len(bounds) - 1:
            for j in range(bounds[k], bounds[k + 1]):
                row_copy(nbase + (j // cap) * t + idxn_ref[0, j // cap, j % cap], 1 - slot, j).start()
            pl.semaphore_signal(fence, 1)
            pl.semaphore_wait(fence, 1)
    for bi in range(bg):
        _store_token_tiles(y_ref.at[bi, 0], y[bi * cap:(bi + 1) * cap])

    @pl.when(last)
    def _():
        def drain(j, carry):
            row_copy(0, 1 - slot, j).wait()
            return carry
        lax.fori_loop(0, m, drain, 0, unroll=8)


def _experts(h2, idx, w13, w2, layer):
    bsz = h2.shape[0]
    t = h2.shape[1] // TILE_ROWS
    d = D_MODEL
    ne, cap = idx.shape[1], idx.shape[2]
    f = w2.shape[2]
    bg = max(1, min(bsz, EXPERT_MIN_ROWS // cap))
    ngroups = bsz // bg
    nsteps = ne * ngroups
    idx3 = idx.transpose(1, 0, 2).reshape(nsteps, bg, cap)

    def nxt(e, g):
        return (jnp.minimum(e * ngroups + g + 1, nsteps - 1), 0, 0)

    return pl.pallas_call(
        functools.partial(_experts_kernel, ngroups=ngroups, bg=bg, t=t, cap=cap, nsteps=nsteps),
        out_shape=jax.ShapeDtypeStruct((bsz, ne, cap * TILE_ROWS, LANES), F32),
        grid=(ne, ngroups),
        in_specs=[
            pl.BlockSpec((1, bg, cap), lambda e, g: (e * ngroups + g, 0, 0), memory_space=pltpu.SMEM),
            pl.BlockSpec((1, bg, cap), nxt, memory_space=pltpu.SMEM),
            pl.BlockSpec(memory_space=pl.ANY),
            pl.BlockSpec((1, 1, d, 2 * f), lambda e, g: (layer, e, 0, 0)),
            pl.BlockSpec((1, 1, f, d), lambda e, g: (layer, e, 0, 0)),
        ],
        out_specs=pl.BlockSpec((bg, 1, cap * TILE_ROWS, LANES), lambda e, g: (g, e, 0, 0)),
        scratch_shapes=[pltpu.VMEM((d, 2 * f), BF16), pltpu.VMEM((f, d), BF16),
                        pltpu.VMEM((2, bg * cap * TILE_ROWS, LANES), F32), pltpu.SemaphoreType.DMA((2,)),
                        pltpu.SemaphoreType.REGULAR],
        compiler_params=_cparams("arbitrary", "arbitrary"),
        name="experts",
    )(idx3, idx3, h2.reshape(bsz * t * TILE_ROWS, LANES), w13, w2)


def _combine_kernel(idx_ref, gate_ref, y_ref, x_ref, g_ref, nw_ref, o_ref, m_scr, *, cap, eg, nsc, tm):
    step = pl.program_id(1)

    @pl.when(step == 0)
    def _():
        m_scr[...] = jnp.zeros_like(m_scr)

    group = 8

    @pl.when(step < nsc)
    def _():
        for ei in range(eg):
            def body(jj, carry):
                base = jj * group
                dst = [pl.ds(pl.multiple_of(idx_ref[ei, 0, base + r] * TILE_ROWS, TILE_ROWS), TILE_ROWS)
                       for r in range(group)]
                acc = [m_scr[d, :] for d in dst]
                for r in range(group):
                    src = pl.ds(pl.multiple_of((base + r) * TILE_ROWS, TILE_ROWS), TILE_ROWS)
                    m_scr[dst[r], :] = acc[r] + gate_ref[ei, 0, base + r] * y_ref[0, ei, src, :]
                return carry
            lax.fori_loop(0, cap // group, body, 0, unroll=2)

    @pl.when(step >= nsc)
    def _():
        base = pl.multiple_of((step - nsc) * (tm * TILE_ROWS), tm * TILE_ROWS)
        m = jnp.concatenate([m_scr[pl.ds(base + k, tm, stride=TILE_ROWS), :] for k in range(TILE_ROWS)], axis=1)
        o_ref[0] = x_ref[0] + g_ref[0] * _rms(m, nw_ref[...])


def _combine(y, idx, gate, x, g2, nw):
    bsz, t, d = x.shape
    _, ne, cap = idx.shape
    eg = max(1, min(ne, COMBINE_MIN_ROWS // cap))
    nsc = ne // eg
    tm = min(t, 512)
    idx3 = idx.reshape(bsz * ne, 1, cap)
    gate3 = gate.reshape(bsz * ne, 1, cap)
    sc = lambda s: jnp.minimum(s, nsc - 1)
    rt = lambda s: jnp.maximum(s - nsc, 0)
    sspec = pl.BlockSpec((eg, 1, cap), lambda b, s: (b * nsc + sc(s), 0, 0), memory_space=pltpu.SMEM)
    tok = pl.BlockSpec((1, tm, d), lambda b, s: (b, rt(s), 0))
    return pl.pallas_call(
        functools.partial(_combine_kernel, cap=cap, eg=eg, nsc=nsc, tm=tm),
        out_shape=jax.ShapeDtypeStruct((bsz, t, d), F32),
        grid=(bsz, nsc + t // tm),
        in_specs=[sspec, sspec,
                  pl.BlockSpec((1, eg, cap * TILE_ROWS, LANES), lambda b, s: (b, sc(s), 0, 0)),
                  tok, pl.BlockSpec((1, 1, d), lambda b, s: (b, 0, 0)), pl.BlockSpec((1, d), lambda b, s: (0, 0))],
        out_specs=tok,
        scratch_shapes=[pltpu.VMEM((t * TILE_ROWS, LANES), F32)],
        compiler_params=_cparams("arbitrary", "arbitrary"),
        name="combine",
    )(idx3, gate3, y, x, g2, nw.reshape(1, d))


def _token_mix_and_moe(stream, z, hf, hb, dft, cdsd, mods, lw):
    sh2, sc2, g1, g2 = mods
    t = stream.shape[1]
    fo = _fourier(z, *dft, *cdsd)
    mid, h2, aff = _merge(z, hf, hb, fo, stream, lw["gnw"], lw["sgu_w"], lw["sgu_bias"], lw["snw"], lw["wb"],
                          lw["wo"], lw["nw1"], lw["nw2"], g1, sh2, sc2, lw["rw_t"], lw["layer"])
    cap = CAPACITY_FACTOR * t // N_EXPERTS
    idx, gate = _select(aff, cap)
    y = _experts(h2, idx, lw["w13"], lw["w2"], lw["layer"])
    return _combine(y, idx, gate, mid, g2, lw["nw3"])


def kernel(x, c, ctx, c_ctx, ada_w, ada_b, norm_w, w_in, hgrn_lb_raw, hgrn_gnorm_w, sgu_w, sgu_b, sgu_norm_w,
           w_branch, w_out, router_w, exp_w13, exp_w2):
    bsz, t, d = x.shape
    tc = ctx.shape[1]
    depth = ada_w.shape[0]
    assert d == D_MODEL and bsz + 1 <= MOD_ROWS and w_in.shape[2] == IN_COLS

    cvec = jnp.zeros((MOD_ROWS, d), F32).at[:bsz].set(c).at[bsz].set(c_ctx)
    mod = _adaln(cvec, ada_w, ada_b).reshape(depth, MOD_ROWS, 6, d)
    dft_lat, dft_ctx = _dft_tables(t), _dft_tables(tc)
    cdsd = _channel_dft()
    zero_state = jnp.zeros((bsz, HEADS, HEAD_DIM, HEAD_DIM), F32)
    w_in_b, w_branch_b, w_out_b = w_in.astype(BF16), w_branch.astype(BF16), w_out.astype(BF16)

    for l in range(depth):
        last = l == depth - 1
        lat = [mod[l, :bsz, k][:, None, :] for k in range(6)]
        cm = [jnp.broadcast_to(mod[l, bsz, k][None, None, :], (bsz, 1, d)) for k in range(6)]
        lw = dict(
            gnw=hgrn_gnorm_w[l].reshape(1, HEAD_DIM),
            sgu_w=sgu_w[l].astype(BF16),
            sgu_bias=jnp.broadcast_to(sgu_b[l][:, :, None], (HEADS, SGU_CHUNK, HEAD_DIM)),
            snw=sgu_norm_w[l],
            wb=w_branch_b, wo=w_out_b, layer=l,
            nw1=norm_w[l, 1].reshape(1, d), nw2=norm_w[l, 2].reshape(1, d), nw3=norm_w[l, 3],
            rw_t=router_w[l].T,
            w13=exp_w13, w2=exp_w2,
        )
        lb_raw = hgrn_lb_raw
        z_ctx = _inproj(ctx, norm_w[l, 0], cm[0], cm[1], w_in_b, l, A_IN if last else IN_COLS)
        hf_c, hb_c, s_f, s_b = _hgrn(z_ctx, lb_raw, zero_state, zero_state, l)
        z_lat = _inproj(x, norm_w[l, 0], lat[0], lat[1], w_in_b, l, IN_COLS)
        hf, hb, _, _ = _hgrn(z_lat, lb_raw, s_f, s_b, l)
        x = _token_mix_and_moe(x, z_lat, hf, hb, dft_lat, cdsd, (lat[3], lat[4], lat[2], lat[5]), lw)
        if not last:
            ctx = _token_mix_and_moe(ctx, z_ctx, hf_c, hb_c, dft_ctx, cdsd, (cm[3], cm[4], cm[2], cm[5]), lw)
    return x
```
